```python
import math
import jax
import jax.numpy as jnp
from jax import lax
import numpy as np

D_MODEL = 1024
BATCH = 16
SEQ = 256
DEPTH = 4
DEC_BATCH = 2
DEC_SEQ = 1024
PAST_LEN = 512

GRID_W = 64
N_DIR = 2
N_MIXERS = 4
D_MIX = D_MODEL
D_GROUP = D_MIX // N_MIXERS
EPS = 1e-6
A_HEADS = 4
A_DK = D_GROUP // A_HEADS
A_CHUNK = 32
B_HEADS = 4
B_HD = D_GROUP // B_HEADS
B_CONV = 4
B_CONV_LEFT = 2
LRU_C = 8.0
C_HEADS = 4
C_HD = D_GROUP // C_HEADS
C_DECAY_RANK = 32
C_ICLR_RANK = 32
C_GATE_RANK = 64
C_LN_EPS = 64e-5
S5_CH = 16
S5_GROUPS = D_GROUP // S5_CH
S5_P = 64
N_EXPERTS = 32
TOP_K = 4
D_FF = D_MODEL
SWIGLU_ALPHA = 1.702
SWIGLU_LIMIT = 7.0
A_COLS = 5 * D_GROUP
B_COLS = 2 * D_GROUP
C_COLS = 3 * D_GROUP + C_DECAY_RANK + C_ICLR_RANK + C_GATE_RANK
D_COLS = D_GROUP
N_IN = A_COLS + B_COLS + C_COLS + D_COLS

kernel_name = 'hybrid_parallel_recurrent_dit_step'


def rms_norm(x, g):
    x32 = x.astype(jnp.float32)
    y = x32 * lax.rsqrt(jnp.mean(x32 * x32, axis=-1, keepdims=True) + EPS)
    return y.astype(x.dtype) * g


def head_layer_norm(x, g, b):
    mu = jnp.mean(x, axis=-1, keepdims=True)
    xc = x - mu
    return xc * lax.rsqrt(jnp.mean(xc * xc, axis=-1, keepdims=True) + C_LN_EPS) * g + b


def flip(t):
    return jnp.flip(t, axis=1)


def linear_combine(e1, e2):
    a1, b1 = e1
    a2, b2 = e2
    return a1 * a2, a2 * b1 + b2


def complex_linear_combine(e1, e2):
    a1r, a1i, b1r, b1i = e1
    a2r, a2i, b2r, b2i = e2
    return (a2r * a1r - a2i * a1i, a2r * a1i + a2i * a1r,
            a2r * b1r - a2i * b1i + b2r, a2r * b1i + a2i * b1r + b2i)


def grid_pos_embed(rows, dtype):
    t = jnp.arange(rows * GRID_W)
    row = (t // GRID_W).astype(jnp.float32)
    col = (t % GRID_W).astype(jnp.float32)
    quarter = D_MODEL // 4
    freq = jnp.exp(-math.log(10000.0) * jnp.arange(quarter, dtype=jnp.float32) / quarter)

    def enc(p):
        ang = p[:, None] * freq[None, :]
        return jnp.concatenate([jnp.sin(ang), jnp.cos(ang)], axis=-1)
    return jnp.concatenate([enc(row), enc(col)], axis=-1).astype(dtype)


def hgrn2_chunk_scan(q, k, v, logf, s0):
    bsz, L, nh, _ = q.shape
    n = L // A_CHUNK

    def chunks(t):
        return t.reshape(bsz, n, A_CHUNK, nh, t.shape[-1]).transpose(1, 0, 3, 2, 4)
    qc, kc, vc, gc = chunks(q), chunks(k), chunks(v), chunks(logf)
    cum = jnp.cumsum(gc, axis=3)
    causal = jnp.tril(jnp.ones((A_CHUNK, A_CHUNK), dtype=bool))[:, :, None]
    diff = cum[..., :, None, :] - cum[..., None, :, :]
    pair_decay = jnp.exp(jnp.where(causal, diff, -jnp.inf))
    scores = jnp.einsum('nbhtk,nbhsk,nbhtsk->nbhts', qc, kc, pair_decay)
    o_intra = jnp.einsum('nbhts,nbhsv->nbhtv', scores, vc)
    q_start = qc * jnp.exp(cum)
    k_end = kc * jnp.exp(cum[..., -1:, :] - cum)
    chunk_decay = jnp.exp(cum[..., -1, :])

    def step(s, xs):
        qs, ke, vv, dec = xs
        o = jnp.einsum('bhtk,bhkv->bhtv', qs, s)
        s = dec[..., None] * s + jnp.einsum('bhtk,bhtv->bhkv', ke, vv)
        return s, o
    s_fin, o_inter = lax.scan(step, s0, (q_start, k_end, vc, chunk_decay))
    o = (o_intra + o_inter).transpose(1, 0, 3, 2, 4).reshape(bsz, L, nh, v.shape[-1])
    return o, s_fin


def hgrn2_mixer(za, lb, norm_g, s0):
    bsz, L, _ = za.shape
    heads = lambda t: t.reshape(bsz, L, A_HEADS, A_DK)
    q, zf_fwd, zf_bwd, i, g = jnp.split(za, 5, axis=-1)
    q, i = heads(jax.nn.silu(q)), heads(i)
    s0 = s0.astype(jnp.float32)
    outs, finals = [], []
    for d, zf in enumerate((zf_fwd, zf_bwd)):
        f = lb[d] + (1.0 - lb[d]) * jax.nn.sigmoid(zf)
        seq = (q, heads(1.0 - f), i, heads(jnp.log(f)))
        if d == 1:
            seq = tuple(flip(t) for t in seq)
        o, s = hgrn2_chunk_scan(*seq, s0[:, d])
        outs.append(flip(o) if d == 1 else o)
        finals.append(s)
    o = rms_norm(outs[0] + outs[1], norm_g.reshape(A_HEADS, A_DK)) * jax.nn.silu(heads(g))
    return o.reshape(bsz, L, D_GROUP), jnp.stack(finals, axis=1)


def centred_dwconv(x, w, b):
    ch = x.shape[-1]
    y = lax.conv_general_dilated(
        x, w.astype(x.dtype)[:, None, :], window_strides=(1,),
        padding=[(B_CONV_LEFT, B_CONV - 1 - B_CONV_LEFT)],
        dimension_numbers=('NWC', 'WIO', 'NWC'), feature_group_count=ch)
    return y + b


def rglru_dir(xc, wa, ba, wx, bx, lam, h0):
    bsz, L, _ = xc.shape
    xh = xc.reshape(bsz, L, B_HEADS, B_HD)
    r = jax.nn.sigmoid(jnp.einsum('blhi,hij->blhj', xh, wa).reshape(bsz, L, D_GROUP) + ba)
    i = jax.nn.sigmoid(jnp.einsum('blhi,hij->blhj', xh, wx).reshape(bsz, L, D_GROUP) + bx)
    log_a = -LRU_C * r * jax.nn.softplus(-lam)
    a = jnp.exp(log_a)
    u = jnp.sqrt(-jnp.expm1(2.0 * log_a)) * (i * xc)
    a_cum, h = lax.associative_scan(linear_combine, (a, u), axis=1)
    return h + a_cum * h0[:, None]


def rglru_mixer(zb, conv_w, conv_b, wa, ba, wx, bx, lam, s0):
    xb, gb = jnp.split(zb, 2, axis=-1)
    xc = centred_dwconv(xb, conv_w, conv_b)
    s0 = s0.astype(jnp.float32)
    outs, finals = [], []
    for d in range(N_DIR):
        h = rglru_dir(flip(xc) if d == 1 else xc, wa[d], ba[d], wx[d], bx[d], lam[d], s0[:, d])
        finals.append(h[:, -1])
        outs.append(flip(h) if d == 1 else h)
    return jax.nn.gelu(gb) * (outs[0] + outs[1]), jnp.stack(finals, axis=1)


def centred_token_shift(t, mu):
    tp = jnp.pad(t, ((0, 0), (1, 1), (0, 0)))
    return t + mu * (0.5 * (tp[:, :-2] + tp[:, 2:]) - t)


def rwkv7_scan(r, w, k, v, kk, a, s0):
    def step(s, xs):
        rt, wt, kt, vt, kkt, at = xs
        sa = jnp.einsum('bhvk,bhk->bhv', s, kkt)
        s = (s * wt[:, :, None, :] - sa[..., None] * (kkt * at)[:, :, None, :]
             + vt[..., None] * kt[:, :, None, :])
        return s, jnp.einsum('bhvk,bhk->bhv', s, rt)
    xs = tuple(jnp.moveaxis(t, 1, 0) for t in (r, w, k, v, kk, a))
    s_fin, y = lax.scan(step, s0, xs)
    return jnp.moveaxis(y, 0, 1), s_fin


def rwkv7_mixer(zc, mu, w0, w_up, a0, a_up, g_up, k_k, k_a, r_k, ln_g, ln_b, s0):
    bsz, L, _ = zc.shape
    heads = lambda t: t.reshape(bsz, L, C_HEADS, C_HD)
    cuts = [D_GROUP, 2 * D_GROUP, 3 * D_GROUP, 3 * D_GROUP + C_DECAY_RANK,
            3 * D_GROUP + C_DECAY_RANK + C_ICLR_RANK]
    r, k, v, zw, zi, zg = jnp.split(zc, cuts, axis=-1)
    r = centred_token_shift(r, mu[0])
    k = centred_token_shift(k, mu[1])
    v = centred_token_shift(v, mu[2])
    kk = heads(k * k_k)
    kk = kk / jnp.maximum(jnp.sqrt(jnp.sum(kk * kk, axis=-1, keepdims=True)), 1e-12)
    gate = jax.nn.sigmoid(zg) @ g_up
    rh, vh = heads(r), heads(v)
    r_k_h = r_k.reshape(C_HEADS, C_HD)
    s0 = s0.astype(jnp.float32)
    outs, bonus, finals = [], [], []
    for d in range(N_DIR):
        w_log = -jax.nn.softplus(-(w0[d] + jnp.tanh(zw) @ w_up[d])) - 0.5
        decay = heads(jnp.exp(-jnp.exp(w_log)))
        a = jax.nn.sigmoid(a0[d] + zi @ a_up[d])
        kd = heads(k * (1.0 + (a - 1.0) * k_a))
        seq = (rh, decay, kd, vh, kk, heads(a))
        if d == 1:
            seq = tuple(flip(t) for t in seq)
        y, s = rwkv7_scan(*seq, s0[:, d])
        outs.append(flip(y) if d == 1 else y)
        finals.append(s)
        bonus.append(jnp.sum(rh * kd * r_k_h, axis=-1, keepdims=True) * vh)
    y = head_layer_norm(outs[0] + outs[1], ln_g.reshape(C_HEADS, C_HD), ln_b.reshape(C_HEADS, C_HD))
    y = y + bonus[0] + bonus[1]
    return y.reshape(bsz, L, D_GROUP) * gate, jnp.stack(finals, axis=1)


def s5_dir(u, a_re, a_im, log_step, b_re, b_im, c_re, c_im, x0):
    step = jnp.exp(log_step)[:, None]
    mag = jnp.exp(step * a_re)
    ab_re, ab_im = mag * jnp.cos(step * a_im), mag * jnp.sin(step * a_im)
    den = a_re * a_re + a_im * a_im
    z_re = ((ab_re - 1.0) * a_re + ab_im * a_im) / den
    z_im = (ab_im * a_re - (ab_re - 1.0) * a_im) / den
    bb_re = z_re[..., None] * b_re - z_im[..., None] * b_im
    bb_im = z_re[..., None] * b_im + z_im[..., None] * b_re
    bu_re = jnp.einsum('gpc,blgc->blgp', bb_re, u)
    bu_im = jnp.einsum('gpc,blgc->blgp', bb_im, u)
    ar = jnp.broadcast_to(ab_re, bu_re.shape)
    ai = jnp.broadcast_to(ab_im, bu_im.shape)
    cr, ci, hr, hi = lax.associative_scan(complex_linear_combine, (ar, ai, bu_re, bu_im), axis=1)
    x0r, x0i = x0[..., 0][:, None], x0[..., 1][:, None]
    xr = hr + cr * x0r - ci * x0i
    xi = hi + cr * x0i + ci * x0r
    y = jnp.einsum('gcp,blgp->blgc', c_re, xr) - jnp.einsum('gcp,blgp->blgc', c_im, xi)
    return y, jnp.stack([xr[:, -1], xi[:, -1]], axis=-1)


def s5_mixer(zd, a_re, a_im, log_step, b_re, b_im, c_re, c_im, d_skip, glu_w, glu_b, s0):
    bsz, L, _ = zd.shape
    u = zd.reshape(bsz, L, S5_GROUPS, S5_CH)
    s0 = s0.astype(jnp.float32)
    outs, finals = [], []
    for d in range(N_DIR):
        y, s = s5_dir(flip(u) if d == 1 else u, a_re[d], a_im[d], log_step[d],
                      b_re[d], b_im[d], c_re[d], c_im[d], s0[:, d])
        outs.append(flip(y) if d == 1 else y)
        finals.append(s)
    y = jax.nn.gelu((outs[0] + outs[1]).reshape(bsz, L, D_GROUP) + d_skip * zd)
    return y * jax.nn.sigmoid(y @ glu_w + glu_b), jnp.stack(finals, axis=1)


def token_mixers(h, states, P, lb, l):
    bsz, L, _ = h.shape
    z = (h @ P['w_in'][l]).astype(jnp.float32)
    za, zb, zc, zd = jnp.split(z, [A_COLS, A_COLS + B_COLS, A_COLS + B_COLS + C_COLS], axis=-1)
    oa, sa = hgrn2_mixer(za, lb, P['hgrn_norm_g'][l], states[0])
    ob, sb = rglru_mixer(zb, P['lru_conv_w'][l], P['lru_conv_b'][l], P['lru_wa'][l], P['lru_ba'][l],
                         P['lru_wx'][l], P['lru_bx'][l], P['lru_lambda'][l], states[1])
    oc, sc = rwkv7_mixer(zc, P['rwkv_mu'][l], P['rwkv_w0'][l], P['rwkv_w_up'][l], P['rwkv_a0'][l],
                         P['rwkv_a_up'][l], P['rwkv_g_up'][l], P['rwkv_k_k'][l], P['rwkv_k_a'][l],
                         P['rwkv_r_k'][l], P['rwkv_ln_g'][l], P['rwkv_ln_b'][l], states[2])
    od, sd = s5_mixer(zd, P['s5_a_re'][l], P['s5_a_im'][l], P['s5_log_step'][l], P['s5_b_re'][l],
                      P['s5_b_im'][l], P['s5_c_re'][l], P['s5_c_im'][l], P['s5_d'][l],
                      P['s5_glu_w'][l], P['s5_glu_b'][l], states[3])
    o = rms_norm(jnp.stack([oa, ob, oc, od], axis=2), P['merge_g'][l].reshape(N_MIXERS, D_GROUP))
    y = o.reshape(bsz, L, D_MIX).astype(h.dtype) @ P['w_out'][l]
    return y, (sa, sb, sc, sd)


def moe_ffn(h, router_w, router_b, w_gu, b_gu, w_down, b_down):
    shp = h.shape
    t = h.reshape(-1, shp[-1])
    logits = (t @ router_w + router_b).astype(jnp.float32)
    top_v, top_i = lax.top_k(logits, TOP_K)
    probs = jax.nn.softmax(top_v, axis=-1)
    combine = jnp.einsum('tk,tke->et', probs,
                         jax.nn.one_hot(top_i, N_EXPERTS, dtype=jnp.float32)).astype(t.dtype)

    def add_expert(acc, xs):
        wgu, bgu, wdn, bdn, gate_e = xs
        gate, up = jnp.split(t @ wgu + bgu, 2, axis=-1)
        gate = jnp.minimum(gate, SWIGLU_LIMIT)
        up = jnp.clip(up, -SWIGLU_LIMIT, SWIGLU_LIMIT)
        act = (up + 1.0) * gate * jax.nn.sigmoid(SWIGLU_ALPHA * gate)
        return acc + gate_e[:, None] * (act @ wdn + bdn), None
    out, _ = lax.scan(add_expert, jnp.zeros_like(t), (w_gu, b_gu, w_down, b_down, combine))
    return out.reshape(shp)


def trunk_layer(x, mod, states, P, lb, l):
    sh1, sc1, g1, sh2, sc2, g2 = jnp.split(mod[:, None, :], 6, axis=-1)
    h = rms_norm(x, P['norm1_g'][l]) * (1.0 + sc1) + sh1
    y, new_states = token_mixers(h, states, P, lb, l)
    x = x + g1 * y
    h = rms_norm(x, P['norm2_g'][l]) * (1.0 + sc2) + sh2
    x = x + g2 * moe_ffn(h, P['router_w'][l], P['router_b'][l], P['moe_w_gu'][l],
                         P['moe_b_gu'][l], P['moe_w_down'][l], P['moe_b_down'][l])
    return x, new_states


def setup_inputs(seed: int = 0) -> dict:
    key = jax.random.key(seed)
    ks = iter(jax.random.split(key, 80))
    f32 = jnp.float32
    nrm = lambda shape, scale: scale * jax.random.normal(next(ks), shape, f32)
    unif = lambda shape, lo, hi: jax.random.uniform(next(ks), shape, f32, lo, hi)
    inp = {}
    inp['x_prompt'] = nrm((BATCH, SEQ, D_MODEL), 1.0)
    inp['x_sample'] = nrm((DEC_BATCH, DEC_SEQ, D_MODEL), 1.0)
    inp['c'] = nrm((DEC_BATCH, D_MODEL), 1.0)
    inp['state_hgrn'] = nrm((DEC_BATCH, DEPTH, N_DIR, A_HEADS, A_DK, A_DK), 0.5)
    inp['state_lru'] = nrm((DEC_BATCH, DEPTH, N_DIR, D_GROUP), 0.5)
    inp['state_rwkv'] = nrm((DEC_BATCH, DEPTH, N_DIR, C_HEADS, C_HD, C_HD), 0.5)
    inp['state_s5'] = nrm((DEC_BATCH, DEPTH, N_DIR, S5_GROUPS, S5_P, 2), 0.5)
    inp['c_ctx'] = nrm((D_MODEL,), 1.0)
    inp['ada_w'] = nrm((DEPTH, D_MODEL, 6 * D_MODEL), 0.5 * D_MODEL ** -0.5)
    inp['ada_b'] = nrm((DEPTH, 6 * D_MODEL), 0.02)
    inp['norm1_g'] = 1.0 + nrm((DEPTH, D_MODEL), 0.02)
    inp['norm2_g'] = 1.0 + nrm((DEPTH, D_MODEL), 0.02)
    inp['w_in'] = nrm((DEPTH, D_MODEL, N_IN), D_MODEL ** -0.5)
    inp['w_out'] = nrm((DEPTH, D_MIX, D_MODEL), D_MIX ** -0.5)
    inp['merge_g'] = 1.0 + nrm((DEPTH, D_MIX), 0.02)
    inp['hgrn_lb_logits'] = nrm((DEPTH, N_DIR, D_GROUP), 0.5)
    inp['hgrn_norm_g'] = 1.0 + nrm((DEPTH, D_GROUP), 0.02)
    inp['lru_conv_w'] = nrm((DEPTH, B_CONV, D_GROUP), 0.5)
    inp['lru_conv_b'] = nrm((DEPTH, D_GROUP), 0.02)
    inp['lru_wa'] = nrm((DEPTH, N_DIR, B_HEADS, B_HD, B_HD), B_HD ** -0.5)
    inp['lru_ba'] = nrm((DEPTH, N_DIR, D_GROUP), 0.02)
    inp['lru_wx'] = nrm((DEPTH, N_DIR, B_HEADS, B_HD, B_HD), B_HD ** -0.5)
    inp['lru_bx'] = nrm((DEPTH, N_DIR, D_GROUP), 0.02)
    a_c = unif((DEPTH, N_DIR, D_GROUP), 0.9, 0.999)
    s = a_c ** (1.0 / LRU_C)
    inp['lru_lambda'] = jnp.log(s) - jnp.log1p(-s)
    inp['rwkv_mu'] = unif((DEPTH, 3, D_GROUP), 0.0, 1.0)
    inp['rwkv_w0'] = unif((DEPTH, N_DIR, D_GROUP), -6.0, 1.0)
    inp['rwkv_w_up'] = nrm((DEPTH, N_DIR, C_DECAY_RANK, D_GROUP), 0.1)
    inp['rwkv_a0'] = nrm((DEPTH, N_DIR, D_GROUP), 0.5)
    inp['rwkv_a_up'] = nrm((DEPTH, N_DIR, C_ICLR_RANK, D_GROUP), 0.1)
    inp['rwkv_g_up'] = nrm((DEPTH, C_GATE_RANK, D_GROUP), C_GATE_RANK ** -0.5)
    inp['rwkv_k_k'] = 0.85 + nrm((DEPTH, D_GROUP), 0.05)
    inp['rwkv_k_a'] = 1.0 + nrm((DEPTH, D_GROUP), 0.05)
    inp['rwkv_r_k'] = nrm((DEPTH, D_GROUP), 0.1)
    inp['rwkv_ln_g'] = 1.0 + nrm((DEPTH, D_GROUP), 0.02)
    inp['rwkv_ln_b'] = nrm((DEPTH, D_GROUP), 0.02)
    inp['s5_a_re'] = -0.5 + nrm((DEPTH, N_DIR, S5_GROUPS, S5_P), 0.01)
    inp['s5_a_im'] = math.pi * jnp.arange(S5_P, dtype=f32) + nrm((DEPTH, N_DIR, S5_GROUPS, S5_P), 0.01)
    inp['s5_log_step'] = unif((DEPTH, N_DIR, S5_GROUPS), math.log(1e-3), math.log(1e-1))
    inp['s5_b_re'] = nrm((DEPTH, N_DIR, S5_GROUPS, S5_P, S5_CH), (2 * S5_CH) ** -0.5)
    inp['s5_b_im'] = nrm((DEPTH, N_DIR, S5_GROUPS, S5_P, S5_CH), (2 * S5_CH) ** -0.5)
    inp['s5_c_re'] = nrm((DEPTH, N_DIR, S5_GROUPS, S5_CH, S5_P), (2 * S5_P) ** -0.5)
    inp['s5_c_im'] = nrm((DEPTH, N_DIR, S5_GROUPS, S5_CH, S5_P), (2 * S5_P) ** -0.5)
    inp['s5_d'] = nrm((DEPTH, D_GROUP), 0.5)
    inp['s5_glu_w'] = nrm((DEPTH, D_GROUP, D_GROUP), D_GROUP ** -0.5)
    inp['s5_glu_b'] = nrm((DEPTH, D_GROUP), 0.02)
    inp['router_w'] = nrm((DEPTH, D_MODEL, N_EXPERTS), D_MODEL ** -0.5)
    inp['router_b'] = nrm((DEPTH, N_EXPERTS), 0.01)
    inp['moe_w_gu'] = nrm((DEPTH, N_EXPERTS, D_MODEL, 2 * D_FF), D_MODEL ** -0.5)
    inp['moe_b_gu'] = nrm((DEPTH, N_EXPERTS, 2 * D_FF), 0.01)
    inp['moe_w_down'] = nrm((DEPTH, N_EXPERTS, D_FF, D_MODEL), D_FF ** -0.5)
    inp['moe_b_down'] = nrm((DEPTH, N_EXPERTS, D_MODEL), 0.01)
    inp['final_norm_g'] = 1.0 + nrm((D_MODEL,), 0.02)
    return inp


def reference(x_prompt, x_sample, c, state_hgrn, state_lru, state_rwkv, state_s5, c_ctx,
              ada_w, ada_b, norm1_g, norm2_g, w_in, w_out, merge_g, hgrn_lb_logits, hgrn_norm_g,
              lru_conv_w, lru_conv_b, lru_wa, lru_ba, lru_wx, lru_bx, lru_lambda,
              rwkv_mu, rwkv_w0, rwkv_w_up, rwkv_a0, rwkv_a_up, rwkv_g_up, rwkv_k_k, rwkv_k_a,
              rwkv_r_k, rwkv_ln_g, rwkv_ln_b, s5_a_re, s5_a_im, s5_log_step, s5_b_re, s5_b_im,
              s5_c_re, s5_c_im, s5_d, s5_glu_w, s5_glu_b, router_w, router_b,
              moe_w_gu, moe_b_gu, moe_w_down, moe_b_down, final_norm_g):
    P = dict(norm1_g=norm1_g, norm2_g=norm2_g, w_in=w_in, w_out=w_out, merge_g=merge_g,
             hgrn_norm_g=hgrn_norm_g, lru_conv_w=lru_conv_w, lru_conv_b=lru_conv_b,
             lru_wa=lru_wa, lru_ba=lru_ba, lru_wx=lru_wx, lru_bx=lru_bx, lru_lambda=lru_lambda,
             rwkv_mu=rwkv_mu, rwkv_w0=rwkv_w0, rwkv_w_up=rwkv_w_up, rwkv_a0=rwkv_a0,
             rwkv_a_up=rwkv_a_up, rwkv_g_up=rwkv_g_up, rwkv_k_k=rwkv_k_k, rwkv_k_a=rwkv_k_a,
             rwkv_r_k=rwkv_r_k, rwkv_ln_g=rwkv_ln_g, rwkv_ln_b=rwkv_ln_b,
             s5_a_re=s5_a_re, s5_a_im=s5_a_im, s5_log_step=s5_log_step, s5_b_re=s5_b_re,
             s5_b_im=s5_b_im, s5_c_re=s5_c_re, s5_c_im=s5_c_im, s5_d=s5_d,
             s5_glu_w=s5_glu_w, s5_glu_b=s5_glu_b, router_w=router_w, router_b=router_b,
             moe_w_gu=moe_w_gu, moe_b_gu=moe_b_gu, moe_w_down=moe_w_down, moe_b_down=moe_b_down)
    lb_cum = jnp.cumsum(jax.nn.softmax(hgrn_lb_logits.astype(jnp.float32), axis=0), axis=0)
    hgrn_lb = lb_cum - lb_cum[0:1]

    bsz = x_prompt.shape[0]
    zero_states = (jnp.zeros((bsz, N_DIR, A_HEADS, A_DK, A_DK), jnp.float32),
                   jnp.zeros((bsz, N_DIR, D_GROUP), jnp.float32),
                   jnp.zeros((bsz, N_DIR, C_HEADS, C_HD, C_HD), jnp.float32),
                   jnp.zeros((bsz, N_DIR, S5_GROUPS, S5_P, 2), jnp.float32))
    xp = x_prompt
    ctx_states = []
    for l in range(DEPTH):
        mod = (jax.nn.silu(c_ctx) @ ada_w[l] + ada_b[l])[None]
        xp, st = trunk_layer(xp, mod, zero_states, P, hgrn_lb[l], l)
        ctx_states.append(st)
    y_prompt = rms_norm(xp, final_norm_g)
    new_state_hgrn = jnp.stack([s[0] for s in ctx_states], axis=1).astype(x_prompt.dtype)
    new_state_lru = jnp.stack([s[1] for s in ctx_states], axis=1).astype(x_prompt.dtype)
    new_state_rwkv = jnp.stack([s[2] for s in ctx_states], axis=1).astype(x_prompt.dtype)
    new_state_s5 = jnp.stack([s[3] for s in ctx_states], axis=1).astype(x_prompt.dtype)

    rows = x_sample.shape[1] // GRID_W
    xs = x_sample + grid_pos_embed(rows, x_sample.dtype)[None]
    silu_c = jax.nn.silu(c)
    for l in range(DEPTH):
        mod = silu_c @ ada_w[l] + ada_b[l]
        st = (state_hgrn[:, l], state_lru[:, l], state_rwkv[:, l], state_s5[:, l])
        xs, _ = trunk_layer(xs, mod, st, P, hgrn_lb[l], l)
    y_sample = rms_norm(xs, final_norm_g)
    return (y_prompt, y_sample, new_state_hgrn, new_state_lru, new_state_rwkv, new_state_s5)
```

```python
import functools
import math

import jax
import jax.numpy as jnp
from jax import lax
from jax.experimental import pallas as pl
from jax.experimental.pallas import tpu as pltpu

F32 = jnp.float32
BF16 = jnp.bfloat16

D_MODEL = 1024
DEPTH = 4
N_CTX_SEQ, CTX_LEN = 16, 256
N_LAT_SEQ, LAT_LEN = 2, 1024
T_CTX = N_CTX_SEQ * CTX_LEN
T_ALL = T_CTX + N_LAT_SEQ * LAT_LEN
GRID_W = 64
DG = 256
HD = 64
EPS = 1e-6
A_COLS, B_COLS, C_COLS, D_COLS = 5 * DG, 2 * DG, 3 * DG + 128, DG
N_IN = A_COLS + B_COLS + C_COLS + D_COLS
HG_CHUNK = 8
RW_CHUNK = 16
RW_GROUP = 64
SC_CHUNK = 8
LRU_C = 8.0
C_LN_EPS = 64e-5
S5_N = 1024
N_EXPERTS = 32
TOP_K = 4
D_FF = 1024
SWIGLU_ALPHA = 1.702
SWIGLU_LIMIT = 7.0
TM = 256
MOE_TM = 1024
MOE_FF = 512
VMEM_LIMIT = 56 * 1024 * 1024
NEG_BIG = -1e30


def _cparams(n_axes):
    return pltpu.CompilerParams(dimension_semantics=("arbitrary",) * n_axes,
                                vmem_limit_bytes=VMEM_LIMIT)


def _iota(shape, axis):
    return lax.broadcasted_iota(jnp.int32, shape, axis)


def _shift(x, d):
    n = x.shape[0]
    d = d % n
    return x if d == 0 else pltpu.roll(x, d, 0)


def _mm(a, b):
    return jnp.dot(a.astype(BF16), b.astype(BF16), preferred_element_type=F32)


def _mm_nt(a, b):
    return lax.dot_general(a.astype(BF16), b.astype(BF16), (((1,), (1,)), ((), ())),
                           preferred_element_type=F32)


def _mm_tn(a, b):
    return lax.dot_general(a, b, (((0,), (0,)), ((), ())), preferred_element_type=F32)


def _block_ones(n, blk):
    r = _iota((n, n), 0) // blk
    c = _iota((n, n), 1) // blk
    return r == c


def _head_sum(x, ones_bf16):
    hi = x.astype(BF16)
    lo = (x - hi.astype(F32)).astype(BF16)
    return (jnp.dot(hi, ones_bf16, preferred_element_type=F32)
            + jnp.dot(lo, ones_bf16, preferred_element_type=F32))


def _sigmoid(x):
    return 1.0 / (1.0 + jnp.exp(-x))


def _silu(x):
    return x * _sigmoid(x)


def _softplus(x):
    return jnp.maximum(x, 0.0) + jnp.log(1.0 + jnp.exp(-jnp.abs(x)))


def _gelu(x):
    c = math.sqrt(2.0 / math.pi)
    return 0.5 * x * (1.0 + jnp.tanh(c * (x + 0.044715 * (x * x * x))))


def _rms(x, eps=EPS):
    return x * lax.rsqrt(jnp.mean(x * x, axis=-1, keepdims=True) + eps)


def _chunk_cumsums(x, chunk):
    j = _iota(x.shape, 0) % chunk
    f, r = x, x
    s = 1
    while s < chunk:
        f = f + jnp.where(j >= s, _shift(f, s), 0.0)
        r = r + jnp.where(j < chunk - s, _shift(r, -s), 0.0)
        s *= 2
    return f, r


def _mod_row(i):
    start = i * TM
    return jnp.where(start < T_CTX, 0, 1 + (start - T_CTX) // LAT_LEN)


def _ada_body(c_ref, w_ref, b_ref, o_ref):
    o_ref[...] = _mm(_silu(c_ref[...]), w_ref[...]) + b_ref[...]


def _ada_table(cvec, ada_w, ada_b):
    nb = 6 * D_MODEL // 1024
    return pl.pallas_call(
        _ada_body,
        grid=(DEPTH, nb),
        in_specs=[pl.BlockSpec((8, D_MODEL), lambda l, j: (0, 0)),
                  pl.BlockSpec((None, D_MODEL, 1024), lambda l, j: (l, 0, j)),
                  pl.BlockSpec((None, 1, 1024), lambda l, j: (l, 0, j))],
        out_specs=pl.BlockSpec((None, 8, 1024), lambda l, j: (l, 0, j)),
        out_shape=jax.ShapeDtypeStruct((DEPTH, 8, 6 * D_MODEL), F32),
        compiler_params=_cparams(2), name="ada_table",
    )(cvec, ada_w, ada_b.reshape(DEPTH, 1, 6 * D_MODEL))


def _embed_body(x_ref, p_ref, o_ref):
    i = pl.program_id(0)

    @pl.when(i * TM < T_CTX)
    def _():
        o_ref[...] = x_ref[...]

    @pl.when(i * TM >= T_CTX)
    def _():
        o_ref[...] = x_ref[...] + p_ref[...]


def _embed(x_cat, pos):
    per_seq = LAT_LEN // TM
    first = T_CTX // TM
    return pl.pallas_call(
        _embed_body,
        grid=(T_ALL // TM,),
        in_specs=[pl.BlockSpec((TM, D_MODEL), lambda i: (i, 0)),
                  pl.BlockSpec((TM, D_MODEL), lambda i: (jnp.maximum(i - first, 0) % per_seq, 0))],
        out_specs=pl.BlockSpec((TM, D_MODEL), lambda i: (i, 0)),
        out_shape=jax.ShapeDtypeStruct((T_ALL, D_MODEL), F32),
        compiler_params=_cparams(1), name="embed",
    )(x_cat, pos)


def _grid_pos_table():
    t = jnp.arange(LAT_LEN)
    row = (t // GRID_W).astype(F32)
    col = (t % GRID_W).astype(F32)
    quarter = D_MODEL // 4
    freq = jnp.exp(-math.log(10000.0) * jnp.arange(quarter, dtype=F32) / quarter)

    def enc(p):
        ang = p[:, None] * freq[None, :]
        return jnp.concatenate([jnp.sin(ang), jnp.cos(ang)], axis=-1)
    return jnp.concatenate([enc(row), enc(col)], axis=-1)


def _inproj_body(x_ref, mod_ref, g_ref, w_ref, za_ref, zb_ref, zc_ref, zd_ref, wbf_ref):
    i = pl.program_id(0)

    @pl.when(i == 0)
    def _():
        wbf_ref[...] = w_ref[...].astype(BF16)

    r = _mod_row(i)
    sh = mod_ref[pl.ds(r, 1), 0:D_MODEL]
    sc = mod_ref[pl.ds(r, 1), D_MODEL:2 * D_MODEL]
    h = _rms(x_ref[...]) * g_ref[...] * (1.0 + sc) + sh
    z = jnp.dot(h.astype(BF16), wbf_ref[...], preferred_element_type=F32)
    za_ref[...] = z[:, 0:A_COLS]
    zb_ref[...] = z[:, A_COLS:A_COLS + B_COLS]
    zc_ref[...] = z[:, A_COLS + B_COLS:A_COLS + B_COLS + C_COLS]
    zd_ref[...] = z[:, A_COLS + B_COLS + C_COLS:N_IN]


def _inproj(x, mod_l, g, w):
    row = lambda i: (i, 0)
    full = lambda i: (0, 0)
    widths = (A_COLS, B_COLS, C_COLS, D_COLS)
    return pl.pallas_call(
        _inproj_body,
        grid=(T_ALL // TM,),
        in_specs=[pl.BlockSpec((TM, D_MODEL), row),
                  pl.BlockSpec((8, 6 * D_MODEL), full),
                  pl.BlockSpec((1, D_MODEL), full),
                  pl.BlockSpec((D_MODEL, N_IN), full)],
        out_specs=[pl.BlockSpec((TM, wd), row) for wd in widths],
        out_shape=[jax.ShapeDtypeStruct((T_ALL, wd), F32) for wd in widths],
        scratch_shapes=[pltpu.VMEM((D_MODEL, N_IN), BF16)],
        compiler_params=_cparams(1), name="inproj",
    )(x, mod_l, g.reshape(1, D_MODEL), w)


def _hgrn_body(layer, seq_len, zero_init, *refs):
    if zero_init:
        za_ref, lbl_ref, ng_ref, o_ref, sf_ref, qs_ref, ke_ref, tot_ref, acc_ref, s_ref = refs
        s0_ref = None
    else:
        za_ref, lbl_ref, ng_ref, s0_ref, o_ref, sf_ref, qs_ref, ke_ref, tot_ref, acc_ref, s_ref = refs
    L = seq_len
    n_chunks = L // HG_CHUNK
    ones = _block_ones(DG, HD).astype(BF16)
    bd = _block_ones(DG, HD).astype(F32)

    lg = lbl_ref[...]
    e = jnp.exp(lg - jnp.max(lg, axis=0, keepdims=True))
    p = e / jnp.sum(e, axis=0, keepdims=True)
    lb = jnp.zeros((1, 2 * DG), F32)
    for m in range(1, layer + 1):
        lb = lb + p[m:m + 1, :]

    q = _silu(za_ref[:, 0:DG])
    v = za_ref[:, 3 * DG:4 * DG]
    j = _iota((L, DG), 0) % HG_CHUNK
    acc_ref[...] = jnp.zeros((L, DG), F32)

    for d in range(2):
        lbd = lb[:, d * DG:(d + 1) * DG]
        f = lbd + (1.0 - lbd) * _sigmoid(za_ref[:, (1 + d) * DG:(2 + d) * DG])
        k = 1.0 - f
        lf = jnp.log(f)
        fc, rc = _chunk_cumsums(lf, HG_CHUNK)
        lam, rest = (fc, rc - lf) if d == 0 else (rc, fc - lf)
        qs_ref[...] = q * jnp.exp(lam)
        ke_ref[...] = k * jnp.exp(rest)
        tot_ref[...] = fc + rc - lf

        o = _head_sum(q * k, ones) * v
        for lag in range(1, HG_CHUNK):
            sgn = lag if d == 0 else -lag
            valid = (j >= lag) if d == 0 else (j < HG_CHUNK - lag)
            ex = jnp.exp(jnp.where(valid, lam - _shift(lam, sgn), NEG_BIG))
            o = o + _head_sum(q * _shift(k, sgn) * ex, ones) * _shift(v, sgn)
        acc_ref[...] += o

        if zero_init:
            s_ref[...] = jnp.zeros((DG, DG), F32)
        else:
            s_ref[...] = s0_ref[d]

        def chunk_step(c, carry, d=d):
            ci = c if d == 0 else n_chunks - 1 - c
            r0 = pl.multiple_of(ci * HG_CHUNK, HG_CHUNK)
            s = s_ref[...]
            acc_ref[pl.ds(r0, HG_CHUNK), :] += _mm_nt(qs_ref[pl.ds(r0, HG_CHUNK), :], s)
            dec = jnp.exp(tot_ref[pl.ds(r0, 1), :])
            upd = _mm_tn(za_ref[pl.ds(r0, HG_CHUNK), 3 * DG:4 * DG], ke_ref[pl.ds(r0, HG_CHUNK), :])
            s_ref[...] = s * dec + upd * bd
            return carry
        lax.fori_loop(0, n_chunks, chunk_step, 0)
        sf_ref[d] = s_ref[...]

    o = acc_ref[...]
    ms = _head_sum(o * o, ones) * (1.0 / HD)
    o_ref[...] = o * lax.rsqrt(ms + EPS) * ng_ref[...] * _silu(za_ref[:, 4 * DG:5 * DG])


def _seq_specs(L, width, first_block):
    return pl.BlockSpec((L, width), lambda b: (first_block + b, 0))


def _lru_body(seq_len, zero_init, *refs):
    if zero_init:
        zb_ref, pv_ref, wa_ref, wx_ref, o_ref, hf_ref, a_ref, u_ref, acc_ref = refs
        h0_ref = None
    else:
        zb_ref, pv_ref, wa_ref, wx_ref, h0_ref, o_ref, hf_ref, a_ref, u_ref, acc_ref = refs
    L = seq_len
    n_chunks = L // SC_CHUNK
    rows = _iota((L, DG), 0)
    j = rows % SC_CHUNK
    xb = zb_ref[:, 0:DG]
    xc = (pv_ref[0:1, :] * jnp.where(rows >= 2, _shift(xb, 2), 0.0)
          + pv_ref[1:2, :] * jnp.where(rows >= 1, _shift(xb, 1), 0.0)
          + pv_ref[2:3, :] * xb
          + pv_ref[3:4, :] * jnp.where(rows < L - 1, _shift(xb, -1), 0.0)
          + pv_ref[4:5, :])
    acc_ref[...] = jnp.zeros((L, DG), F32)
    for d in range(2):
        ba, bx, lam = pv_ref[5 + d:6 + d, :], pv_ref[7 + d:8 + d, :], pv_ref[9 + d:10 + d, :]
        r = _sigmoid(_mm(xc, wa_ref[d]) + ba)
        i = _sigmoid(_mm(xc, wx_ref[d]) + bx)
        log_a = -LRU_C * r * _softplus(-lam)
        a = jnp.exp(log_a)
        u = jnp.sqrt(1.0 - jnp.exp(2.0 * log_a)) * (i * xc)
        s = 1
        while s < SC_CHUNK:
            sgn = s if d == 0 else -s
            valid = (j >= s) if d == 0 else (j < SC_CHUNK - s)
            u = u + a * jnp.where(valid, _shift(u, sgn), 0.0)
            a = a * jnp.where(valid, _shift(a, sgn), 1.0)
            s *= 2
        a_ref[...] = a
        u_ref[...] = u
        h0 = jnp.zeros((1, DG), F32) if zero_init else h0_ref[d:d + 1, :]

        def chunk_step(c, carry, d=d):
            ci = c if d == 0 else n_chunks - 1 - c
            r0 = pl.multiple_of(ci * SC_CHUNK, SC_CHUNK)
            h = u_ref[pl.ds(r0, SC_CHUNK), :] + a_ref[pl.ds(r0, SC_CHUNK), :] * carry
            acc_ref[pl.ds(r0, SC_CHUNK), :] += h
            return h[SC_CHUNK - 1:SC_CHUNK, :] if d == 0 else h[0:1, :]
        hf_ref[d:d + 1, :] = lax.fori_loop(0, n_chunks, chunk_step, h0)
    o_ref[...] = _gelu(zb_ref[:, DG:2 * DG]) * acc_ref[...]


def _mixer_call(body, name, zero_init, z, z_width, params, state0, state_shape, n_scratch_rows,
                extra_scratch, o_prev):
    L, nseq = (CTX_LEN, N_CTX_SEQ) if zero_init else (LAT_LEN, N_LAT_SEQ)
    first = 0 if zero_init else T_CTX // LAT_LEN
    in_specs = [_seq_specs(L, z_width, first)]
    args = [z]
    for p in params:
        in_specs.append(pl.BlockSpec(p.shape, lambda b, nd=p.ndim: (0,) * nd))
        args.append(p)
    nstate = len(state_shape)
    state_spec = pl.BlockSpec((None,) + state_shape, lambda b: (b,) + (0,) * nstate)
    aliases = {}
    kern = functools.partial(body, L, zero_init)
    if not zero_init:
        in_specs.append(state_spec)
        args.append(state0)
        in_specs.append(pl.BlockSpec(memory_space=pl.ANY))
        args.append(o_prev)
        aliases = {len(args) - 1: 0}
        n_in = len(args)
        inner = kern
        kern = lambda *refs: inner(*refs[:n_in - 1], *refs[n_in:])
    return pl.pallas_call(
        kern,
        grid=(nseq,),
        in_specs=in_specs,
        out_specs=[_seq_specs(L, DG, first), state_spec],
        out_shape=[jax.ShapeDtypeStruct((T_ALL, DG), F32),
                   jax.ShapeDtypeStruct((nseq,) + state_shape, F32)],
        scratch_shapes=[pltpu.VMEM((L, w), F32) for w in n_scratch_rows] + list(extra_scratch),
        input_output_aliases=aliases,
        compiler_params=_cparams(1), name=name + ("_ctx" if zero_init else "_lat"),
    )(*args)


RW_SCRATCH = 14


def _rwkv_body(seq_len, zero_init, *refs):
    if zero_init:
        zc_ref, pv_ref, wup_ref, aup_ref, gup_ref, o_ref, sf_ref = refs[:7]
        s0_ref = None
        scr = refs[7:]
    else:
        zc_ref, pv_ref, wup_ref, aup_ref, gup_ref, s0_ref, o_ref, sf_ref = refs[:8]
        scr = refs[8:]
    (kt_ref, rt_ref, kh_ref, bh_ref, kb_ref, bb_ref, tot_ref, v_ref, w_ref, u_ref, rh_ref, yh_ref,
     acc_ref, yf_ref, s_ref) = scr
    L = seq_len
    n_chunks = L // RW_CHUNK
    n_groups = L // RW_GROUP
    ones = _block_ones(DG, HD).astype(BF16)
    bd = _block_ones(DG, HD).astype(F32)
    rows = _iota((L, DG), 0)

    def pvr(i):
        return pv_ref[i:i + 1, :]

    def tshift(t, mu):
        prev = jnp.where(rows >= 1, _shift(t, 1), 0.0)
        nxt = jnp.where(rows < L - 1, _shift(t, -1), 0.0)
        return t + mu * (0.5 * (prev + nxt) - t)

    r = tshift(zc_ref[:, 0:DG], pvr(0))
    k = tshift(zc_ref[:, DG:2 * DG], pvr(1))
    v = tshift(zc_ref[:, 2 * DG:3 * DG], pvr(2))
    tail = zc_ref[:, 3 * DG:3 * DG + 128]
    kk = k * pvr(7)
    kk = kk / jnp.maximum(jnp.sqrt(_head_sum(kk * kk, ones)), 1e-12)
    gate = _mm(_sigmoid(tail), gup_ref[...])
    tanh_tail = jnp.tanh(tail)
    v_ref[...] = v
    acc_ref[...] = jnp.zeros((L, DG), F32)

    n_big = 4 * RW_GROUP
    ri, ci = _iota((n_big, n_big), 0), _iota((n_big, n_big), 1)
    same = ((ri // RW_GROUP) == (ci // RW_GROUP)) & (((ri % RW_GROUP) // RW_CHUNK) == ((ci % RW_GROUP) // RW_CHUNK))
    eye = (ri == ci).astype(F32)
    lane_head = _iota((RW_GROUP, DG), 1) // HD

    def stack_heads(x):
        return jnp.concatenate([jnp.where(lane_head == h, x, 0.0) for h in range(4)], axis=0)

    def fold_heads(x):
        return (x[0:RW_GROUP] + x[RW_GROUP:2 * RW_GROUP]
                + x[2 * RW_GROUP:3 * RW_GROUP] + x[3 * RW_GROUP:4 * RW_GROUP])

    for d in range(2):
        w_log = -_softplus(-(pvr(3 + d) + _mm(tanh_tail, wup_ref[d]))) - 0.5
        lw = -jnp.exp(w_log)
        a = _sigmoid(pvr(5 + d) + _mm(tail, aup_ref[d]))
        kd = k * (1.0 + (a - 1.0) * pvr(8))
        beta = kk * a
        acc_ref[...] += _head_sum(r * kd * pvr(9), ones) * v
        fc, rc = _chunk_cumsums(lw, RW_CHUNK)
        lam, rest = (fc, rc - lw) if d == 0 else (rc, fc - lw)
        kt_ref[...] = kk * jnp.exp(lam - lw)
        rt_ref[...] = r * jnp.exp(lam)
        inv = jnp.exp(-lam)
        kh_ref[...] = kd * inv
        bh_ref[...] = beta * inv
        ex_rest = jnp.exp(rest)
        kb_ref[...] = kd * ex_rest
        bb_ref[...] = beta * ex_rest
        tot_ref[...] = fc + rc - lw

        earlier = (ci < ri) if d == 0 else (ci > ri)
        m_strict = (same & earlier).astype(F32)
        m_incl = (same & (earlier | (ri == ci))).astype(F32)

        def local_step(g, carry):
            g0 = pl.multiple_of(g * RW_GROUP, RW_GROUP)
            sl = pl.ds(g0, RW_GROUP)
            lhs_k = stack_heads(kt_ref[sl, :])
            lhs_r = stack_heads(rt_ref[sl, :])
            rhs_b = jnp.concatenate([bh_ref[sl, :]] * 4, axis=0)
            rhs_k = jnp.concatenate([kh_ref[sl, :]] * 4, axis=0)
            v_big = stack_heads(v_ref[sl, :])
            n_mat = m_strict * _mm_nt(lhs_k, rhs_b)
            m_mat = m_strict * _mm_nt(lhs_k, rhs_k)
            a_rb = m_incl * _mm_nt(lhs_r, rhs_b)
            a_rk = m_incl * _mm_nt(lhs_r, rhs_k)
            pw = -n_mat
            t_mat = eye + pw
            s = 2
            while s < RW_CHUNK:
                pw = _mm(pw, pw)
                t_mat = t_mat + _mm(t_mat, pw)
                s *= 2
            w_big = _mm(t_mat, lhs_k)
            u_big = _mm(t_mat, _mm(m_mat, v_big))
            w_ref[sl, :] = fold_heads(w_big)
            u_ref[sl, :] = fold_heads(u_big)
            rh_ref[sl, :] = fold_heads(lhs_r - _mm(a_rb, w_big))
            yh_ref[sl, :] = fold_heads(_mm(a_rk, v_big) - _mm(a_rb, u_big))
            return carry
        lax.fori_loop(0, n_groups, local_step, 0)

        if zero_init:
            s_ref[...] = jnp.zeros((DG, DG), F32)
        else:
            s_ref[...] = s0_ref[d]

        def chunk_step(c, carry, d=d):
            cidx = c if d == 0 else n_chunks - 1 - c
            r0 = pl.multiple_of(cidx * RW_CHUNK, RW_CHUNK)
            sl = pl.ds(r0, RW_CHUNK)
            s = s_ref[...]
            both = _mm_nt(jnp.concatenate([w_ref[sl, :], rh_ref[sl, :]], axis=0), s)
            sa = both[0:RW_CHUNK] + u_ref[sl, :]
            yh_ref[sl, :] = both[RW_CHUNK:2 * RW_CHUNK] + yh_ref[sl, :]
            upd = _mm_tn(jnp.concatenate([v_ref[sl, :], sa], axis=0),
                         jnp.concatenate([kb_ref[sl, :], -bb_ref[sl, :]], axis=0))
            s_ref[...] = s * jnp.exp(tot_ref[pl.ds(r0, 1), :]) + upd * bd
            return carry
        lax.fori_loop(0, n_chunks, chunk_step, 0)
        sf_ref[d] = s_ref[...]
        if d == 0:
            yf_ref[...] = yh_ref[...]
    y = yf_ref[...] + yh_ref[...]
    mu = _head_sum(y, ones) * (1.0 / HD)
    yc = y - mu
    var = _head_sum(yc * yc, ones) * (1.0 / HD)
    y = yc * lax.rsqrt(var + C_LN_EPS) * pvr(10) + pvr(11)
    o_ref[...] = (y + acc_ref[...]) * gate


S5_BLK = 256


def _s5_body(seq_len, zero_init, *refs):
    if zero_init:
        zd_ref, pa_ref, pv_ref, bre_ref, bim_ref, cre_ref, cim_ref, glu_ref, o_ref, xf_ref = refs[:10]
        x0_ref = None
        scr = refs[10:]
    else:
        (zd_ref, pa_ref, pv_ref, bre_ref, bim_ref, cre_ref, cim_ref, glu_ref, x0_ref,
         o_ref, xf_ref) = refs[:11]
        scr = refs[11:]
    xr_ref, xi_ref, acc_ref = scr
    L = seq_len
    n_chunks = L // SC_CHUNK
    n_blk = L // S5_BLK
    jb = _iota((S5_BLK, S5_N), 0) % SC_CHUNK
    j8 = _iota((SC_CHUNK, S5_N), 0)

    def cmul(ar, ai, br, bi):
        return ar * br - ai * bi, ar * bi + ai * br

    for d in range(2):
        a_re, a_im, ls = pa_ref[d, 0:1, :], pa_ref[d, 1:2, :], pa_ref[d, 2:3, :]
        step = jnp.exp(ls)
        mag = jnp.exp(step * a_re)
        p1r, p1i = mag * jnp.cos(step * a_im), mag * jnp.sin(step * a_im)
        den = a_re * a_re + a_im * a_im
        zr = ((p1r - 1.0) * a_re + p1i * a_im) / den
        zi = (p1i * a_re - (p1r - 1.0) * a_im) / den
        p2r, p2i = cmul(p1r, p1i, p1r, p1i)
        p4r, p4i = cmul(p2r, p2i, p2r, p2i)
        powers = ((1, p1r, p1i), (2, p2r, p2i), (4, p4r, p4i))
        pwr = jnp.broadcast_to(p1r, (SC_CHUNK, S5_N))
        pwi = jnp.broadcast_to(p1i, (SC_CHUNK, S5_N))
        for s, _, _ in powers:
            sgn = s if d == 0 else -s
            valid = (j8 >= s) if d == 0 else (j8 < SC_CHUNK - s)
            nr, ni = cmul(pwr, pwi, _shift(pwr, sgn), _shift(pwi, sgn))
            pwr, pwi = jnp.where(valid, nr, pwr), jnp.where(valid, ni, pwi)

        def blk_step(bi, carry, d=d, zr=zr, zi=zi, powers=powers):
            sl = pl.ds(pl.multiple_of(bi * S5_BLK, S5_BLK), S5_BLK)
            u = zd_ref[sl, :]
            m1 = _mm(u, bre_ref[d])
            m2 = _mm(u, bim_ref[d])
            xr = zr * m1 - zi * m2
            xi = zr * m2 + zi * m1
            for s, pr, pi in powers:
                sgn = s if d == 0 else -s
                valid = (jb >= s) if d == 0 else (jb < SC_CHUNK - s)
                sr = jnp.where(valid, _shift(xr, sgn), 0.0)
                si = jnp.where(valid, _shift(xi, sgn), 0.0)
                xr, xi = xr + pr * sr - pi * si, xi + pr * si + pi * sr
            xr_ref[sl, :] = xr
            xi_ref[sl, :] = xi
            return carry
        lax.fori_loop(0, n_blk, blk_step, 0)

        if zero_init:
            c0 = (jnp.zeros((1, S5_N), F32), jnp.zeros((1, S5_N), F32))
        else:
            c0 = (x0_ref[d, 0:1, :], x0_ref[d, 1:2, :])

        def chunk_step(c, carry, d=d, pwr=pwr, pwi=pwi):
            ci = c if d == 0 else n_chunks - 1 - c
            sl = pl.ds(pl.multiple_of(ci * SC_CHUNK, SC_CHUNK), SC_CHUNK)
            cr, cim = carry
            hr = xr_ref[sl, :] + pwr * cr - pwi * cim
            hi = xi_ref[sl, :] + pwr * cim + pwi * cr
            xr_ref[sl, :] = hr
            xi_ref[sl, :] = hi
            last = SC_CHUNK - 1 if d == 0 else 0
            return hr[last:last + 1, :], hi[last:last + 1, :]
        fr, fi = lax.fori_loop(0, n_chunks, chunk_step, c0)
        xf_ref[d, 0:1, :] = fr
        xf_ref[d, 1:2, :] = fi

        def out_step(bi, carry, d=d):
            sl = pl.ds(pl.multiple_of(bi * S5_BLK, S5_BLK), S5_BLK)
            y = _mm(xr_ref[sl, :], cre_ref[d]) - _mm(xi_ref[sl, :], cim_ref[d])
            if d == 0:
                acc_ref[sl, :] = y
            else:
                acc_ref[sl, :] += y
            return carry
        lax.fori_loop(0, n_blk, out_step, 0)

    y = _gelu(acc_ref[...] + pv_ref[0:1, :] * zd_ref[...])
    o_ref[...] = y * _sigmoid(_mm(y, glu_ref[...]) + pv_ref[1:2, :])


def _outproj_body(oa_ref, ob_ref, oc_ref, od_ref, x_ref, mod_ref, mg_ref, w_ref, g2_ref, rw_ref, rb_ref,
                  x1_ref, h2_ref, comb_ref, wbf_ref):
    i = pl.program_id(0)

    @pl.when(i == 0)
    def _():
        wbf_ref[...] = w_ref[...].astype(BF16)

    r = _mod_row(i)

    def mod(n):
        return mod_ref[pl.ds(r, 1), n * D_MODEL:(n + 1) * D_MODEL]
    y = jnp.zeros((TM, D_MODEL), F32)
    for n, ref in enumerate((oa_ref, ob_ref, oc_ref, od_ref)):
        on = _rms(ref[...]) * mg_ref[:, n * DG:(n + 1) * DG]
        y = y + jnp.dot(on.astype(BF16), wbf_ref[n * DG:(n + 1) * DG, :], preferred_element_type=F32)
    x1 = x_ref[...] + mod(2) * y
    x1_ref[...] = x1
    h2 = _rms(x1) * g2_ref[...] * (1.0 + mod(4)) + mod(3)
    h2_ref[...] = h2.astype(BF16)
    logits = jnp.dot(h2, rw_ref[...], precision=lax.Precision.HIGHEST,
                     preferred_element_type=F32) + rb_ref[...]
    lane = _iota((TM, N_EXPERTS), 1).astype(F32)
    work = logits
    top = jnp.max(work, axis=-1, keepdims=True)
    comb = jnp.zeros((TM, N_EXPERTS), F32)
    denom = jnp.zeros((TM, 1), F32)
    for _ in range(TOP_K):
        m = jnp.max(work, axis=-1, keepdims=True)
        idx = jnp.min(jnp.where(work == m, lane, float(N_EXPERTS)), axis=-1, keepdims=True)
        sel = lane == idx
        pe = jnp.exp(m - top)
        comb = comb + jnp.where(sel, pe, 0.0)
        denom = denom + pe
        work = jnp.where(sel, -jnp.inf, work)
    comb_ref[...] = comb / denom


def _outproj(o_mix, x, mod_l, mg, w_out, g2, rw, rb):
    row = lambda i: (i, 0)
    full = lambda i: (0, 0)
    return pl.pallas_call(
        _outproj_body,
        grid=(T_ALL // TM,),
        in_specs=[pl.BlockSpec((TM, DG), row)] * 4 + [
            pl.BlockSpec((TM, D_MODEL), row),
            pl.BlockSpec((8, 6 * D_MODEL), full),
            pl.BlockSpec((1, D_MODEL), full),
            pl.BlockSpec((D_MODEL, D_MODEL), full),
            pl.BlockSpec((1, D_MODEL), full),
            pl.BlockSpec((D_MODEL, N_EXPERTS), full),
            pl.BlockSpec((1, N_EXPERTS), full)],
        out_specs=[pl.BlockSpec((TM, D_MODEL), row), pl.BlockSpec((TM, D_MODEL), row),
                   pl.BlockSpec((TM, N_EXPERTS), row)],
        out_shape=[jax.ShapeDtypeStruct((T_ALL, D_MODEL), F32),
                   jax.ShapeDtypeStruct((T_ALL, D_MODEL), BF16),
                   jax.ShapeDtypeStruct((T_ALL, N_EXPERTS), F32)],
        scratch_shapes=[pltpu.VMEM((D_MODEL, D_MODEL), BF16)],
        compiler_params=_cparams(1), name="outproj_router",
    )(*o_mix, x, mod_l, mg.reshape(1, D_MODEL), w_out, g2.reshape(1, D_MODEL), rw,
      rb.reshape(1, N_EXPERTS))


def _moe_body(h_ref, comb_ref, x_ref, mod_ref, wg_ref, wu_ref, bg_ref, bu_ref, wd_ref, bd_ref, o_ref):
    i, e, j = pl.program_id(0), pl.program_id(1), pl.program_id(2)
    last_e, last_j = pl.num_programs(1) - 1, pl.num_programs(2) - 1

    @pl.when((e == 0) & (j == 0))
    def _():
        o_ref[...] = jnp.zeros((MOE_TM, D_MODEL), F32)

    lane = _iota((MOE_TM, N_EXPERTS), 1)
    ce = jnp.sum(jnp.where(lane == e, comb_ref[...], 0.0), axis=-1, keepdims=True)
    h = h_ref[...]
    gate = jnp.dot(h, wg_ref[...].astype(BF16), preferred_element_type=F32) + bg_ref[...]
    up = jnp.dot(h, wu_ref[...].astype(BF16), preferred_element_type=F32) + bu_ref[...]
    gate = jnp.minimum(gate, SWIGLU_LIMIT)
    up = jnp.clip(up, -SWIGLU_LIMIT, SWIGLU_LIMIT)
    act = (up + 1.0) * gate * _sigmoid(SWIGLU_ALPHA * gate)
    y = jnp.dot(act.astype(BF16), wd_ref[...].astype(BF16), preferred_element_type=F32)

    @pl.when(j == 0)
    def _():
        o_ref[...] += ce * (y + bd_ref[...])

    @pl.when(j != 0)
    def _():
        o_ref[...] += ce * y

    @pl.when((e == last_e) & (j == last_j))
    def _():
        r = jnp.where(i * MOE_TM < T_CTX, 0, 1 + (i * MOE_TM - T_CTX) // LAT_LEN)
        g2 = mod_ref[pl.ds(r, 1), 5 * D_MODEL:6 * D_MODEL]
        o_ref[...] = x_ref[...] + g2 * o_ref[...]


def _moe(h2, comb, x1, mod_l, w_gu, b_gu, w_dn, b_dn):
    nj = D_FF // MOE_FF
    tok = lambda i, e, j: (i, 0)
    return pl.pallas_call(
        _moe_body,
        grid=(T_ALL // MOE_TM, N_EXPERTS, nj),
        in_specs=[pl.BlockSpec((MOE_TM, D_MODEL), tok),
                  pl.BlockSpec((MOE_TM, N_EXPERTS), tok),
                  pl.BlockSpec((MOE_TM, D_MODEL), tok),
                  pl.BlockSpec((8, 6 * D_MODEL), lambda i, e, j: (0, 0)),
                  pl.BlockSpec((None, D_MODEL, MOE_FF), lambda i, e, j: (e, 0, j)),
                  pl.BlockSpec((None, D_MODEL, MOE_FF), lambda i, e, j: (e, 0, nj + j)),
                  pl.BlockSpec((None, 1, MOE_FF), lambda i, e, j: (e, 0, j)),
                  pl.BlockSpec((None, 1, MOE_FF), lambda i, e, j: (e, 0, nj + j)),
                  pl.BlockSpec((None, MOE_FF, D_MODEL), lambda i, e, j: (e, j, 0)),
                  pl.BlockSpec((None, 1, D_MODEL), lambda i, e, j: (e, 0, 0))],
        out_specs=pl.BlockSpec((MOE_TM, D_MODEL), tok),
        out_shape=jax.ShapeDtypeStruct((T_ALL, D_MODEL), F32),
        compiler_params=_cparams(3), name="moe",
    )(h2, comb, x1, mod_l, w_gu, w_gu, b_gu.reshape(N_EXPERTS, 1, 2 * D_FF),
      b_gu.reshape(N_EXPERTS, 1, 2 * D_FF), w_dn, b_dn.reshape(N_EXPERTS, 1, D_MODEL))


def _final_body(x_ref, g_ref, o_ref):
    o_ref[...] = _rms(x_ref[...]) * g_ref[...]


def _final_norm(x, g):
    return pl.pallas_call(
        _final_body,
        grid=(T_ALL // TM,),
        in_specs=[pl.BlockSpec((TM, D_MODEL), lambda i: (i, 0)),
                  pl.BlockSpec((1, D_MODEL), lambda i: (0, 0))],
        out_specs=pl.BlockSpec((TM, D_MODEL), lambda i: (i, 0)),
        out_shape=jax.ShapeDtypeStruct((T_ALL, D_MODEL), F32),
        compiler_params=_cparams(1), name="final_norm",
    )(x, g.reshape(1, D_MODEL))


def _to_block_diag(s):
    eye = jnp.eye(4, dtype=s.dtype)
    out = jnp.einsum('hg,...hab->...hagb', eye, s)
    return out.reshape(s.shape[:-3] + (DG, DG))


def _from_block_diag(m):
    blocks = m.reshape(m.shape[:-2] + (4, HD, 4, HD))
    return jnp.stack([blocks[..., h, :, h, :] for h in range(4)], axis=-3)


def _pad_rows(w, first, total=128):
    pad = [(0, 0)] * (w.ndim - 2) + [(first, total - first - w.shape[-2]), (0, 0)]
    return jnp.pad(w, pad)


def kernel(x_prompt, x_sample, c, state_hgrn, state_lru, state_rwkv, state_s5, c_ctx, ada_w, ada_b, norm1_g, norm2_g, w_in, w_out, merge_g, hgrn_lb_logits, hgrn_norm_g, lru_conv_w, lru_conv_b, lru_wa, lru_ba, lru_wx, lru_bx, lru_lambda, rwkv_mu, rwkv_w0, rwkv_w_up, rwkv_a0, rwkv_a_up, rwkv_g_up, rwkv_k_k, rwkv_k_a, rwkv_r_k, rwkv_ln_g, rwkv_ln_b, s5_a_re, s5_a_im, s5_log_step, s5_b_re, s5_b_im, s5_c_re, s5_c_im, s5_d, s5_glu_w, s5_glu_b, router_w, router_b, moe_w_gu, moe_b_gu, moe_w_down, moe_b_down, final_norm_g):
    cvec = jnp.concatenate([c_ctx[None], c, jnp.zeros((5, D_MODEL), F32)], axis=0)
    x_cat = jnp.concatenate([x_prompt.reshape(T_CTX, D_MODEL),
                             x_sample.reshape(N_LAT_SEQ * LAT_LEN, D_MODEL)], axis=0)
    lbl = hgrn_lb_logits.reshape(DEPTH, 2 * DG)
    hg_s0 = _to_block_diag(jnp.swapaxes(state_hgrn, -1, -2))
    rw_s0 = _to_block_diag(state_rwkv)
    s5_x0 = jnp.moveaxis(state_s5.reshape(N_LAT_SEQ, DEPTH, 2, S5_N, 2), -1, -2)
    lru_pv = jnp.concatenate([lru_conv_w, lru_conv_b[:, None], lru_ba, lru_bx, lru_lambda,
                              jnp.zeros((DEPTH, 5, DG), F32)], axis=1)
    lru_wa_bd = _to_block_diag(lru_wa)
    lru_wx_bd = _to_block_diag(lru_wx)
    rw_pv = jnp.concatenate([rwkv_mu, rwkv_w0, rwkv_a0, rwkv_k_k[:, None], rwkv_k_a[:, None],
                             rwkv_r_k[:, None], rwkv_ln_g[:, None], rwkv_ln_b[:, None],
                             jnp.zeros((DEPTH, 4, DG), F32)], axis=1)
    rw_wup = _pad_rows(rwkv_w_up, 0)
    rw_aup = _pad_rows(rwkv_a_up, 32)
    rw_gup = _pad_rows(rwkv_g_up, 64)
    s5_pa = jnp.stack([s5_a_re.reshape(DEPTH, 2, S5_N), s5_a_im.reshape(DEPTH, 2, S5_N),
                       jnp.repeat(s5_log_step, 64, axis=-1)], axis=2)
    s5_pa = jnp.pad(s5_pa, ((0, 0), (0, 0), (0, 5), (0, 0)))
    s5_pv = jnp.concatenate([s5_d[:, None], s5_glu_b[:, None], jnp.zeros((DEPTH, 6, DG), F32)], axis=1)
    eye16 = jnp.eye(16, dtype=F32)
    s5_bre = jnp.einsum('gh,ldgpc->ldgchp', eye16, s5_b_re).reshape(DEPTH, 2, DG, S5_N)
    s5_bim = jnp.einsum('gh,ldgpc->ldgchp', eye16, s5_b_im).reshape(DEPTH, 2, DG, S5_N)
    s5_cre = jnp.einsum('gh,ldgcp->ldgphc', eye16, s5_c_re).reshape(DEPTH, 2, S5_N, DG)
    s5_cim = jnp.einsum('gh,ldgcp->ldgphc', eye16, s5_c_im).reshape(DEPTH, 2, S5_N, DG)

    mod = _ada_table(cvec, ada_w, ada_b)
    x = _embed(x_cat, _grid_pos_table())

    new_hgrn, new_lru, new_rwkv, new_s5 = [], [], [], []
    for l in range(DEPTH):
        za, zb, zc, zd = _inproj(x, mod[l], norm1_g[l], w_in[l])

        hg_body = functools.partial(_hgrn_body, l)
        hg_params = (lbl, hgrn_norm_g[l].reshape(1, DG))
        hg_extra = (pltpu.VMEM((DG, DG), F32),)
        oa, sa = _mixer_call(hg_body, "hgrn", True, za, A_COLS, hg_params, None, (2, DG, DG),
                             (DG,) * 4, hg_extra, None)
        oa, _ = _mixer_call(hg_body, "hgrn", False, za, A_COLS, hg_params, hg_s0[:, l], (2, DG, DG),
                            (DG,) * 4, hg_extra, oa)

        lru_params = (lru_pv[l], lru_wa_bd[l], lru_wx_bd[l])
        ob, sb = _mixer_call(_lru_body, "lru", True, zb, B_COLS, lru_params, None, (2, DG),
                             (DG, DG, DG), (), None)
        ob, _ = _mixer_call(_lru_body, "lru", False, zb, B_COLS, lru_params, state_lru[:, l], (2, DG),
                            (DG, DG, DG), (), ob)

        rw_params = (rw_pv[l], rw_wup[l], rw_aup[l], rw_gup[l])
        rw_extra = (pltpu.VMEM((DG, DG), F32),)
        oc, sc = _mixer_call(_rwkv_body, "rwkv", True, zc, C_COLS, rw_params, None, (2, DG, DG),
                             (DG,) * RW_SCRATCH, rw_extra, None)
        oc, _ = _mixer_call(_rwkv_body, "rwkv", False, zc, C_COLS, rw_params, rw_s0[:, l], (2, DG, DG),
                            (DG,) * RW_SCRATCH, rw_extra, oc)

        s5_params = (s5_pa[l], s5_pv[l], s5_bre[l], s5_bim[l], s5_cre[l], s5_cim[l], s5_glu_w[l])
        od, sd = _mixer_call(_s5_body, "s5", True, zd, D_COLS, s5_params, None, (2, 2, S5_N),
                             (S5_N, S5_N, DG), (), None)
        od, _ = _mixer_call(_s5_body, "s5", False, zd, D_COLS, s5_params, s5_x0[:, l], (2, 2, S5_N),
                            (S5_N, S5_N, DG), (), od)

        x1, h2, comb = _outproj((oa, ob, oc, od), x, mod[l], merge_g[l], w_out[l], norm2_g[l],
                                router_w[l], router_b[l])
        x = _moe(h2, comb, x1, mod[l], moe_w_gu[l], moe_b_gu[l], moe_w_down[l], moe_b_down[l])
        new_hgrn.append(sa)
        new_lru.append(sb)
        new_rwkv.append(sc)
        new_s5.append(sd)

    y = _final_norm(x, final_norm_g)
    y_prompt = y[:T_CTX].reshape(N_CTX_SEQ, CTX_LEN, D_MODEL)
    y_sample = y[T_CTX:].reshape(N_LAT_SEQ, LAT_LEN, D_MODEL)
    st_hgrn = jnp.swapaxes(_from_block_diag(jnp.stack(new_hgrn, axis=1)), -1, -2)
    st_lru = jnp.stack(new_lru, axis=1)
    st_rwkv = _from_block_diag(jnp.stack(new_rwkv, axis=1))
    st_s5 = jnp.moveaxis(jnp.stack(new_s5, axis=1), -2, -1).reshape(N_CTX_SEQ, DEPTH, 2, 16, 64, 2)
    return (y_prompt, y_sample, st_hgrn, st_lru, st_rwkv, st_s5)
```

```python
import functools
import math

import jax
import jax.numpy as jnp
from jax import lax
from jax.experimental import pallas as pl
from jax.experimental.pallas import tpu as pltpu

F32 = jnp.float32
BF16 = jnp.bfloat16

D_MODEL = 1024
DEPTH = 4
N_CTX_SEQ, CTX_LEN = 16, 256
N_LAT_SEQ, LAT_LEN = 2, 1024
T_CTX = N_CTX_SEQ * CTX_LEN
T_ALL = T_CTX + N_LAT_SEQ * LAT_LEN
GRID_W = 64
DG = 256
HD = 64
EPS = 1e-6
A_COLS, B_COLS, C_COLS, D_COLS = 5 * DG, 2 * DG, 3 * DG + 128, DG
N_IN = A_COLS + B_COLS + C_COLS + D_COLS
HG_CHUNK = 8
RW_CHUNK = 64
RW_GROUP = 64
SC_CHUNK = 8
LRU_C = 8.0
C_LN_EPS = 64e-5
S5_N = 1024
N_EXPERTS = 32
TOP_K = 4
D_FF = 1024
SWIGLU_ALPHA = 1.702
SWIGLU_LIMIT = 7.0
TM = 256
MOE_PART = 1024
N_PARTS = T_ALL // MOE_PART
MOE_SLOTS = MOE_PART * TOP_K
MOE_TILE = 128
MOE_TILES = MOE_SLOTS // MOE_TILE + N_EXPERTS
VMEM_LIMIT = 56 * 1024 * 1024
NEG_BIG = -1e30


def _cparams(n_axes):
    return pltpu.CompilerParams(dimension_semantics=("arbitrary",) * n_axes,
                                vmem_limit_bytes=VMEM_LIMIT)


def _iota(shape, axis):
    return lax.broadcasted_iota(jnp.int32, shape, axis)


def _shift(x, d):
    n = x.shape[0]
    d = d % n
    return x if d == 0 else pltpu.roll(x, d, 0)


def _mm(a, b):
    return jnp.dot(a.astype(BF16), b.astype(BF16), preferred_element_type=F32)


def _mm_nt(a, b):
    return lax.dot_general(a.astype(BF16), b.astype(BF16), (((1,), (1,)), ((), ())),
                           preferred_element_type=F32)


def _mm_tn(a, b):
    return lax.dot_general(a, b, (((0,), (0,)), ((), ())), preferred_element_type=F32)


def _block_ones(n, blk):
    r = _iota((n, n), 0) // blk
    c = _iota((n, n), 1) // blk
    return r == c


def _head_sum(x, ones_bf16):
    hi = x.astype(BF16)
    lo = (x - hi.astype(F32)).astype(BF16)
    return (jnp.dot(hi, ones_bf16, preferred_element_type=F32)
            + jnp.dot(lo, ones_bf16, preferred_element_type=F32))


def _sigmoid(x):
    return 1.0 / (1.0 + jnp.exp(-x))


def _silu(x):
    return x * _sigmoid(x)


def _softplus(x):
    return jnp.maximum(x, 0.0) + jnp.log(1.0 + jnp.exp(-jnp.abs(x)))


def _gelu(x):
    c = math.sqrt(2.0 / math.pi)
    return 0.5 * x * (1.0 + jnp.tanh(c * (x + 0.044715 * (x * x * x))))


def _rms(x, eps=EPS):
    return x * lax.rsqrt(jnp.mean(x * x, axis=-1, keepdims=True) + eps)


def _chunk_cumsums(x, chunk):
    j = _iota(x.shape, 0) % chunk
    f, r = x, x
    s = 1
    while s < chunk:
        f = f + jnp.where(j >= s, _shift(f, s), 0.0)
        r = r + jnp.where(j < chunk - s, _shift(r, -s), 0.0)
        s *= 2
    return f, r


def _mod_row(i):
    start = i * TM
    return jnp.where(start < T_CTX, 0, 1 + (start - T_CTX) // LAT_LEN)


def _ada_body(c_ref, w_ref, b_ref, o_ref):
    o_ref[...] = _mm(_silu(c_ref[...]), w_ref[...]) + b_ref[...]


def _ada_table(cvec, ada_w, ada_b):
    nb = 6 * D_MODEL // 1024
    return pl.pallas_call(
        _ada_body,
        grid=(DEPTH, nb),
        in_specs=[pl.BlockSpec((8, D_MODEL), lambda l, j: (0, 0)),
                  pl.BlockSpec((None, D_MODEL, 1024), lambda l, j: (l, 0, j)),
                  pl.BlockSpec((None, 1, 1024), lambda l, j: (l, 0, j))],
        out_specs=pl.BlockSpec((None, 8, 1024), lambda l, j: (l, 0, j)),
        out_shape=jax.ShapeDtypeStruct((DEPTH, 8, 6 * D_MODEL), F32),
        compiler_params=_cparams(2), name="ada_table",
    )(cvec, ada_w, ada_b.reshape(DEPTH, 1, 6 * D_MODEL))


def _embed_body(x_ref, p_ref, o_ref):
    i = pl.program_id(0)

    @pl.when(i * TM < T_CTX)
    def _():
        o_ref[...] = x_ref[...]

    @pl.when(i * TM >= T_CTX)
    def _():
        o_ref[...] = x_ref[...] + p_ref[...]


def _embed(x_cat, pos):
    per_seq = LAT_LEN // TM
    first = T_CTX // TM
    return pl.pallas_call(
        _embed_body,
        grid=(T_ALL // TM,),
        in_specs=[pl.BlockSpec((TM, D_MODEL), lambda i: (i, 0)),
                  pl.BlockSpec((TM, D_MODEL), lambda i: (jnp.maximum(i - first, 0) % per_seq, 0))],
        out_specs=pl.BlockSpec((TM, D_MODEL), lambda i: (i, 0)),
        out_shape=jax.ShapeDtypeStruct((T_ALL, D_MODEL), F32),
        compiler_params=_cparams(1), name="embed",
    )(x_cat, pos)


def _grid_pos_table():
    t = jnp.arange(LAT_LEN)
    row = (t // GRID_W).astype(F32)
    col = (t % GRID_W).astype(F32)
    quarter = D_MODEL // 4
    freq = jnp.exp(-math.log(10000.0) * jnp.arange(quarter, dtype=F32) / quarter)

    def enc(p):
        ang = p[:, None] * freq[None, :]
        return jnp.concatenate([jnp.sin(ang), jnp.cos(ang)], axis=-1)
    return jnp.concatenate([enc(row), enc(col)], axis=-1)


def _mod_chunk(mod_ref, row, n):
    return mod_ref[pl.ds(row, 1), n * D_MODEL:(n + 1) * D_MODEL]


def _mod_spec(layer):
    return pl.BlockSpec((None, 8, 6 * D_MODEL), lambda i: (layer, 0, 0))


def _inproj_body(has_moe, *refs):
    if has_moe:
        (x1_ref, m_ref, pmod_ref, mod_ref, g_ref, w_ref,
         x_ref, za_ref, zb_ref, zc_ref, zd_ref, wbf_ref) = refs
    else:
        x_ref, mod_ref, g_ref, w_ref, za_ref, zb_ref, zc_ref, zd_ref, wbf_ref = refs
    i = pl.program_id(0)

    @pl.when(i == 0)
    def _():
        wbf_ref[...] = w_ref[...].astype(BF16)

    r = _mod_row(i)
    if has_moe:
        x = x1_ref[...] + _mod_chunk(pmod_ref, r, 5) * m_ref[...]
        x_ref[...] = x
    else:
        x = x_ref[...]
    h = _rms(x) * g_ref[...] * (1.0 + _mod_chunk(mod_ref, r, 1)) + _mod_chunk(mod_ref, r, 0)
    z = jnp.dot(h.astype(BF16), wbf_ref[...], preferred_element_type=F32)
    za_ref[...] = z[:, 0:A_COLS]
    zb_ref[...] = z[:, A_COLS:A_COLS + B_COLS]
    zc_ref[...] = z[:, A_COLS + B_COLS:A_COLS + B_COLS + C_COLS]
    zd_ref[...] = z[:, A_COLS + B_COLS + C_COLS:N_IN]


def _inproj(layer, x, moe_out, mod, g, w_in):
    has_moe = moe_out is not None
    row = lambda i: (i, 0)
    widths = (A_COLS, B_COLS, C_COLS, D_COLS)
    tok_spec = pl.BlockSpec((TM, D_MODEL), row)
    in_specs = [tok_spec]
    args = [x]
    if has_moe:
        in_specs += [tok_spec, _mod_spec(layer - 1)]
        args += [moe_out, mod]
    in_specs += [_mod_spec(layer), pl.BlockSpec((1, D_MODEL), lambda i: (0, 0)),
                 pl.BlockSpec((None, D_MODEL, N_IN), lambda i: (layer, 0, 0))]
    args += [mod, g.reshape(1, D_MODEL), w_in]
    out_specs = [pl.BlockSpec((TM, wd), row) for wd in widths]
    out_shape = [jax.ShapeDtypeStruct((T_ALL, wd), F32) for wd in widths]
    if has_moe:
        out_specs = [tok_spec] + out_specs
        out_shape = [jax.ShapeDtypeStruct((T_ALL, D_MODEL), F32)] + out_shape
    outs = pl.pallas_call(
        functools.partial(_inproj_body, has_moe),
        grid=(T_ALL // TM,),
        in_specs=in_specs, out_specs=out_specs, out_shape=out_shape,
        scratch_shapes=[pltpu.VMEM((D_MODEL, N_IN), BF16)],
        compiler_params=_cparams(1), name="inproj",
    )(*args)
    return list(outs) if has_moe else [x] + list(outs)


def _hgrn_body(layer, seq_len, zero_init, *refs):
    if zero_init:
        za_ref, lbl_ref, ng_ref, o_ref, sf_ref, qs_ref, ke_ref, tot_ref, acc_ref, s_ref = refs
        s0_ref = None
    else:
        za_ref, lbl_ref, ng_ref, s0_ref, o_ref, sf_ref, qs_ref, ke_ref, tot_ref, acc_ref, s_ref = refs
    L = seq_len
    n_chunks = L // HG_CHUNK
    ones = _block_ones(DG, HD).astype(BF16)
    bd = _block_ones(DG, HD).astype(F32)

    lg = lbl_ref[...]
    e = jnp.exp(lg - jnp.max(lg, axis=0, keepdims=True))
    p = e / jnp.sum(e, axis=0, keepdims=True)
    lb = jnp.zeros((1, 2 * DG), F32)
    for m in range(1, layer + 1):
        lb = lb + p[m:m + 1, :]

    q = _silu(za_ref[:, 0:DG])
    v = za_ref[:, 3 * DG:4 * DG]
    j = _iota((L, DG), 0) % HG_CHUNK
    acc_ref[...] = jnp.zeros((L, DG), F32)

    for d in range(2):
        lbd = lb[:, d * DG:(d + 1) * DG]
        f = lbd + (1.0 - lbd) * _sigmoid(za_ref[:, (1 + d) * DG:(2 + d) * DG])
        k = 1.0 - f
        lf = jnp.log(f)
        fc, rc = _chunk_cumsums(lf, HG_CHUNK)
        lam, rest = (fc, rc - lf) if d == 0 else (rc, fc - lf)
        qs_ref[...] = q * jnp.exp(lam)
        ke_ref[...] = k * jnp.exp(rest)
        tot_ref[...] = fc + rc - lf

        o = _head_sum(q * k, ones) * v
        for lag in range(1, HG_CHUNK):
            sgn = lag if d == 0 else -lag
            valid = (j >= lag) if d == 0 else (j < HG_CHUNK - lag)
            ex = jnp.exp(jnp.where(valid, lam - _shift(lam, sgn), NEG_BIG))
            o = o + _mm(q * _shift(k, sgn) * ex, ones) * _shift(v, sgn)
        acc_ref[...] += o

        if zero_init:
            s_ref[...] = jnp.zeros((DG, DG), F32)
        else:
            s_ref[...] = s0_ref[d]

        def chunk_step(c, carry, d=d):
            ci = c if d == 0 else n_chunks - 1 - c
            r0 = pl.multiple_of(ci * HG_CHUNK, HG_CHUNK)
            s = s_ref[...]
            acc_ref[pl.ds(r0, HG_CHUNK), :] += _mm_nt(qs_ref[pl.ds(r0, HG_CHUNK), :], s)
            dec = jnp.exp(tot_ref[pl.ds(r0, 1), :])
            upd = _mm_tn(za_ref[pl.ds(r0, HG_CHUNK), 3 * DG:4 * DG], ke_ref[pl.ds(r0, HG_CHUNK), :])
            s_ref[...] = s * dec + upd * bd
            return carry
        lax.fori_loop(0, n_chunks, chunk_step, 0, unroll=4)
        sf_ref[d] = s_ref[...]

    o = acc_ref[...]
    ms = _head_sum(o * o, ones) * (1.0 / HD)
    o_ref[...] = o * lax.rsqrt(ms + EPS) * ng_ref[...] * _silu(za_ref[:, 4 * DG:5 * DG])


def _seq_specs(L, width, first_block):
    return pl.BlockSpec((L, width), lambda b: (first_block + b, 0))


def _lru_body(seq_len, zero_init, *refs):
    if zero_init:
        zb_ref, pv_ref, wa_ref, wx_ref, o_ref, hf_ref, a_ref, u_ref, acc_ref = refs
        h0_ref = None
    else:
        zb_ref, pv_ref, wa_ref, wx_ref, h0_ref, o_ref, hf_ref, a_ref, u_ref, acc_ref = refs
    L = seq_len
    n_chunks = L // SC_CHUNK
    rows = _iota((L, DG), 0)
    j = rows % SC_CHUNK
    xb = zb_ref[:, 0:DG]
    xc = (pv_ref[0:1, :] * jnp.where(rows >= 2, _shift(xb, 2), 0.0)
          + pv_ref[1:2, :] * jnp.where(rows >= 1, _shift(xb, 1), 0.0)
          + pv_ref[2:3, :] * xb
          + pv_ref[3:4, :] * jnp.where(rows < L - 1, _shift(xb, -1), 0.0)
          + pv_ref[4:5, :])
    acc_ref[...] = jnp.zeros((L, DG), F32)
    for d in range(2):
        ba, bx, lam = pv_ref[5 + d:6 + d, :], pv_ref[7 + d:8 + d, :], pv_ref[9 + d:10 + d, :]
        r = _sigmoid(_mm(xc, wa_ref[d]) + ba)
        i = _sigmoid(_mm(xc, wx_ref[d]) + bx)
        log_a = -LRU_C * r * _softplus(-lam)
        a = jnp.exp(log_a)
        u = jnp.sqrt(1.0 - jnp.exp(2.0 * log_a)) * (i * xc)
        s = 1
        while s < SC_CHUNK:
            sgn = s if d == 0 else -s
            valid = (j >= s) if d == 0 else (j < SC_CHUNK - s)
            u = u + a * jnp.where(valid, _shift(u, sgn), 0.0)
            a = a * jnp.where(valid, _shift(a, sgn), 1.0)
            s *= 2
        a_ref[...] = a
        u_ref[...] = u
        h0 = jnp.zeros((1, DG), F32) if zero_init else h0_ref[d:d + 1, :]

        def chunk_step(c, carry, d=d):
            ci = c if d == 0 else n_chunks - 1 - c
            r0 = pl.multiple_of(ci * SC_CHUNK, SC_CHUNK)
            h = u_ref[pl.ds(r0, SC_CHUNK), :] + a_ref[pl.ds(r0, SC_CHUNK), :] * carry
            acc_ref[pl.ds(r0, SC_CHUNK), :] += h
            return h[SC_CHUNK - 1:SC_CHUNK, :] if d == 0 else h[0:1, :]
        hf_ref[d:d + 1, :] = lax.fori_loop(0, n_chunks, chunk_step, h0)
    o_ref[...] = _gelu(zb_ref[:, DG:2 * DG]) * acc_ref[...]


def _mixer_call(body, name, zero_init, z, z_width, params, state0, state_shape, n_scratch_rows,
                extra_scratch, o_prev):
    L, nseq = (CTX_LEN, N_CTX_SEQ) if zero_init else (LAT_LEN, N_LAT_SEQ)
    first = 0 if zero_init else T_CTX // LAT_LEN
    in_specs = [_seq_specs(L, z_width, first)]
    args = [z]
    for p in params:
        in_specs.append(pl.BlockSpec(p.shape, lambda b, nd=p.ndim: (0,) * nd))
        args.append(p)
    nstate = len(state_shape)
    state_spec = pl.BlockSpec((None,) + state_shape, lambda b: (b,) + (0,) * nstate)
    aliases = {}
    kern = functools.partial(body, L, zero_init)
    if not zero_init:
        in_specs.append(state_spec)
        args.append(state0)
        in_specs.append(pl.BlockSpec(memory_space=pl.ANY))
        args.append(o_prev)
        aliases = {len(args) - 1: 0}
        n_in = len(args)
        inner = kern
        kern = lambda *refs: inner(*refs[:n_in - 1], *refs[n_in:])
    return pl.pallas_call(
        kern,
        grid=(nseq,),
        in_specs=in_specs,
        out_specs=[_seq_specs(L, DG, first), state_spec],
        out_shape=[jax.ShapeDtypeStruct((T_ALL, DG), F32),
                   jax.ShapeDtypeStruct((nseq,) + state_shape, F32)],
        scratch_shapes=[pltpu.VMEM((L, w), F32) for w in n_scratch_rows] + list(extra_scratch),
        input_output_aliases=aliases,
        compiler_params=_cparams(1), name=name + ("_ctx" if zero_init else "_lat"),
    )(*args)


RW_SCRATCH = 14


def _rwkv_body(seq_len, zero_init, *refs):
    if zero_init:
        zc_ref, pv_ref, wup_ref, aup_ref, gup_ref, o_ref, sf_ref = refs[:7]
        s0_ref = None
        scr = refs[7:]
    else:
        zc_ref, pv_ref, wup_ref, aup_ref, gup_ref, s0_ref, o_ref, sf_ref = refs[:8]
        scr = refs[8:]
    (kt_ref, rt_ref, kh_ref, bh_ref, kb_ref, bb_ref, tot_ref, v_ref, w_ref, u_ref, rh_ref, yh_ref,
     acc_ref, yf_ref, s_ref) = scr
    L = seq_len
    n_chunks = L // RW_CHUNK
    n_groups = L // RW_GROUP
    ones = _block_ones(DG, HD).astype(BF16)
    bd = _block_ones(DG, HD).astype(F32)
    rows = _iota((L, DG), 0)

    def pvr(i):
        return pv_ref[i:i + 1, :]

    def tshift(t, mu):
        prev = jnp.where(rows >= 1, _shift(t, 1), 0.0)
        nxt = jnp.where(rows < L - 1, _shift(t, -1), 0.0)
        return t + mu * (0.5 * (prev + nxt) - t)

    r = tshift(zc_ref[:, 0:DG], pvr(0))
    k = tshift(zc_ref[:, DG:2 * DG], pvr(1))
    v = tshift(zc_ref[:, 2 * DG:3 * DG], pvr(2))
    tail = zc_ref[:, 3 * DG:3 * DG + 128]
    kk = k * pvr(7)
    kk = kk / jnp.maximum(jnp.sqrt(_head_sum(kk * kk, ones)), 1e-12)
    gate = _mm(_sigmoid(tail), gup_ref[...])
    tanh_tail = jnp.tanh(tail)
    v_ref[...] = v
    acc_ref[...] = jnp.zeros((L, DG), F32)

    n_big = 4 * RW_GROUP
    ri, ci = _iota((n_big, n_big), 0), _iota((n_big, n_big), 1)
    same = ((ri // RW_GROUP) == (ci // RW_GROUP)) & (((ri % RW_GROUP) // RW_CHUNK) == ((ci % RW_GROUP) // RW_CHUNK))
    eye = (ri == ci).astype(F32)
    lane_head = _iota((RW_GROUP, DG), 1) // HD

    def pair_blocks(b):
        return ((ri // (2 * b)) == (ci // (2 * b))) & ((ri // b) != (ci // b))

    def stack_heads(x):
        return jnp.concatenate([jnp.where(lane_head == h, x, 0.0) for h in range(4)], axis=0)

    def fold_heads(x):
        return (x[0:RW_GROUP] + x[RW_GROUP:2 * RW_GROUP]
                + x[2 * RW_GROUP:3 * RW_GROUP] + x[3 * RW_GROUP:4 * RW_GROUP])

    for d in range(2):
        w_log = -_softplus(-(pvr(3 + d) + _mm(tanh_tail, wup_ref[d]))) - 0.5
        lw = -jnp.exp(w_log)
        a = _sigmoid(pvr(5 + d) + _mm(tail, aup_ref[d]))
        kd = k * (1.0 + (a - 1.0) * pvr(8))
        beta = kk * a
        acc_ref[...] += _head_sum(r * kd * pvr(9), ones) * v
        fc, rc = _chunk_cumsums(lw, RW_CHUNK)
        lam, rest = (fc, rc - lw) if d == 0 else (rc, fc - lw)
        kt_ref[...] = kk * jnp.exp(lam - lw)
        rt_ref[...] = r * jnp.exp(lam)
        inv = jnp.exp(-lam)
        kh_ref[...] = kd * inv
        bh_ref[...] = beta * inv
        ex_rest = jnp.exp(rest)
        kb_ref[...] = kd * ex_rest
        bb_ref[...] = beta * ex_rest
        tot_ref[...] = fc + rc - lw

        earlier = (ci < ri) if d == 0 else (ci > ri)
        m_strict = (same & earlier).astype(F32)
        m_incl = (same & (earlier | (ri == ci))).astype(F32)

        def local_step(g, carry):
            g0 = pl.multiple_of(g * RW_GROUP, RW_GROUP)
            sl = pl.ds(g0, RW_GROUP)
            lhs_k = stack_heads(kt_ref[sl, :])
            lhs_r = stack_heads(rt_ref[sl, :])
            rhs_b = jnp.concatenate([bh_ref[sl, :]] * 4, axis=0)
            rhs_k = jnp.concatenate([kh_ref[sl, :]] * 4, axis=0)
            v_big = stack_heads(v_ref[sl, :])
            n_mat = m_strict * _mm_nt(lhs_k, rhs_b)
            m_mat = m_strict * _mm_nt(lhs_k, rhs_k)
            a_rb = m_incl * _mm_nt(lhs_r, rhs_b)
            a_rk = m_incl * _mm_nt(lhs_r, rhs_k)
            t_mat = eye - jnp.where(pair_blocks(1), n_mat, 0.0)
            b = 2
            while b < RW_CHUNK:
                t_mat = t_mat - _mm(t_mat, _mm(jnp.where(pair_blocks(b), n_mat, 0.0), t_mat))
                b *= 2
            w_big = _mm(t_mat, lhs_k)
            u_big = _mm(t_mat, _mm(m_mat, v_big))
            w_ref[sl, :] = fold_heads(w_big)
            u_ref[sl, :] = fold_heads(u_big)
            rh_ref[sl, :] = fold_heads(lhs_r - _mm(a_rb, w_big))
            yh_ref[sl, :] = fold_heads(_mm(a_rk, v_big) - _mm(a_rb, u_big))
            return carry
        lax.fori_loop(0, n_groups, local_step, 0, unroll=2)

        if zero_init:
            s_ref[...] = jnp.zeros((DG, DG), F32)
        else:
            s_ref[...] = s0_ref[d]

        def chunk_step(c, carry, d=d):
            cidx = c if d == 0 else n_chunks - 1 - c
            r0 = pl.multiple_of(cidx * RW_CHUNK, RW_CHUNK)
            sl = pl.ds(r0, RW_CHUNK)
            s = s_ref[...]
            yh_ref[sl, :] = _mm_nt(rh_ref[sl, :], s) + yh_ref[sl, :]
            bb = bb_ref[sl, :].astype(BF16)
            phi = _mm_tn(w_ref[sl, :].astype(BF16), bb)
            delta = _mm_tn(jnp.concatenate([v_ref[sl, :], u_ref[sl, :]], axis=0).astype(BF16),
                           jnp.concatenate([kb_ref[sl, :].astype(BF16), -bb], axis=0))
            s_ref[...] = s * jnp.exp(tot_ref[pl.ds(r0, 1), :]) + bd * (delta - _mm(s, phi))
            return carry
        lax.fori_loop(0, n_chunks, chunk_step, 0, unroll=2)
        sf_ref[d] = s_ref[...]
        if d == 0:
            yf_ref[...] = yh_ref[...]
    y = yf_ref[...] + yh_ref[...]
    mu = _head_sum(y, ones) * (1.0 / HD)
    yc = y - mu
    var = _head_sum(yc * yc, ones) * (1.0 / HD)
    y = yc * lax.rsqrt(var + C_LN_EPS) * pvr(10) + pvr(11)
    o_ref[...] = (y + acc_ref[...]) * gate


S5_BLK = 256


def _s5_body(seq_len, zero_init, *refs):
    if zero_init:
        zd_ref, pa_ref, pv_ref, bre_ref, bim_ref, cre_ref, cim_ref, glu_ref, o_ref, xf_ref = refs[:10]
        x0_ref = None
        scr = refs[10:]
    else:
        (zd_ref, pa_ref, pv_ref, bre_ref, bim_ref, cre_ref, cim_ref, glu_ref, x0_ref,
         o_ref, xf_ref) = refs[:11]
        scr = refs[11:]
    xr_ref, xi_ref, acc_ref = scr
    L = seq_len
    n_chunks = L // SC_CHUNK
    n_blk = L // S5_BLK
    jb = _iota((S5_BLK, S5_N), 0) % SC_CHUNK
    j8 = _iota((SC_CHUNK, S5_N), 0)

    def cmul(ar, ai, br, bi):
        return ar * br - ai * bi, ar * bi + ai * br

    for d in range(2):
        a_re, a_im, ls = pa_ref[d, 0:1, :], pa_ref[d, 1:2, :], pa_ref[d, 2:3, :]
        step = jnp.exp(ls)
        mag = jnp.exp(step * a_re)
        p1r, p1i = mag * jnp.cos(step * a_im), mag * jnp.sin(step * a_im)
        den = a_re * a_re + a_im * a_im
        zr = ((p1r - 1.0) * a_re + p1i * a_im) / den
        zi = (p1i * a_re - (p1r - 1.0) * a_im) / den
        p2r, p2i = cmul(p1r, p1i, p1r, p1i)
        p4r, p4i = cmul(p2r, p2i, p2r, p2i)
        powers = ((1, p1r, p1i), (2, p2r, p2i), (4, p4r, p4i))
        pwr = jnp.broadcast_to(p1r, (SC_CHUNK, S5_N))
        pwi = jnp.broadcast_to(p1i, (SC_CHUNK, S5_N))
        for s, _, _ in powers:
            sgn = s if d == 0 else -s
            valid = (j8 >= s) if d == 0 else (j8 < SC_CHUNK - s)
            nr, ni = cmul(pwr, pwi, _shift(pwr, sgn), _shift(pwi, sgn))
            pwr, pwi = jnp.where(valid, nr, pwr), jnp.where(valid, ni, pwi)

        def blk_step(bi, carry, d=d, zr=zr, zi=zi, powers=powers):
            sl = pl.ds(pl.multiple_of(bi * S5_BLK, S5_BLK), S5_BLK)
            u = zd_ref[sl, :]
            m1 = _mm(u, bre_ref[d])
            m2 = _mm(u, bim_ref[d])
            xr = zr * m1 - zi * m2
            xi = zr * m2 + zi * m1
            for s, pr, pi in powers:
                sgn = s if d == 0 else -s
                valid = (jb >= s) if d == 0 else (jb < SC_CHUNK - s)
                sr = jnp.where(valid, _shift(xr, sgn), 0.0)
                si = jnp.where(valid, _shift(xi, sgn), 0.0)
                xr, xi = xr + pr * sr - pi * si, xi + pr * si + pi * sr
            xr_ref[sl, :] = xr
            xi_ref[sl, :] = xi
            return carry
        lax.fori_loop(0, n_blk, blk_step, 0)

        if zero_init:
            c0 = (jnp.zeros((1, S5_N), F32), jnp.zeros((1, S5_N), F32))
        else:
            c0 = (x0_ref[d, 0:1, :], x0_ref[d, 1:2, :])

        def chunk_step(c, carry, d=d, pwr=pwr, pwi=pwi):
            ci = c if d == 0 else n_chunks - 1 - c
            sl = pl.ds(pl.multiple_of(ci * SC_CHUNK, SC_CHUNK), SC_CHUNK)
            cr, cim = carry
            hr = xr_ref[sl, :] + pwr * cr - pwi * cim
            hi = xi_ref[sl, :] + pwr * cim + pwi * cr
            xr_ref[sl, :] = hr
            xi_ref[sl, :] = hi
            last = SC_CHUNK - 1 if d == 0 else 0
            return hr[last:last + 1, :], hi[last:last + 1, :]
        fr, fi = lax.fori_loop(0, n_chunks, chunk_step, c0)
        xf_ref[d, 0:1, :] = fr
        xf_ref[d, 1:2, :] = fi

        def out_step(bi, carry, d=d):
            sl = pl.ds(pl.multiple_of(bi * S5_BLK, S5_BLK), S5_BLK)
            y = _mm(xr_ref[sl, :], cre_ref[d]) - _mm(xi_ref[sl, :], cim_ref[d])
            if d == 0:
                acc_ref[sl, :] = y
            else:
                acc_ref[sl, :] += y
            return carry
        lax.fori_loop(0, n_blk, out_step, 0)

    y = _gelu(acc_ref[...] + pv_ref[0:1, :] * zd_ref[...])
    o_ref[...] = y * _sigmoid(_mm(y, glu_ref[...]) + pv_ref[1:2, :])


def _outproj_body(oa_ref, ob_ref, oc_ref, od_ref, x_ref, mod_ref, mg_ref, w_ref, g2_ref, rw_ref, rb_ref,
                  x1_ref, h2_ref, route_ref, wbf_ref):
    i = pl.program_id(0)

    @pl.when(i == 0)
    def _():
        wbf_ref[...] = w_ref[...].astype(BF16)

    r = _mod_row(i)
    y = jnp.zeros((TM, D_MODEL), F32)
    for n, ref in enumerate((oa_ref, ob_ref, oc_ref, od_ref)):
        on = _rms(ref[...]) * mg_ref[:, n * DG:(n + 1) * DG]
        y = y + jnp.dot(on.astype(BF16), wbf_ref[n * DG:(n + 1) * DG, :], preferred_element_type=F32)
    x1 = x_ref[...] + _mod_chunk(mod_ref, r, 2) * y
    x1_ref[...] = x1
    h2 = _rms(x1) * g2_ref[...] * (1.0 + _mod_chunk(mod_ref, r, 4)) + _mod_chunk(mod_ref, r, 3)
    h2_ref[...] = h2.astype(BF16)
    logits = jnp.dot(h2, rw_ref[...], precision=lax.Precision.HIGHEST,
                     preferred_element_type=F32) + rb_ref[...]
    lane = _iota((TM, N_EXPERTS), 1).astype(F32)
    col = _iota((TM, 2 * TOP_K), 1)
    work = logits
    top = jnp.max(work, axis=-1, keepdims=True)
    route = jnp.zeros((TM, 2 * TOP_K), F32)
    denom = jnp.zeros((TM, 1), F32)
    for k in range(TOP_K):
        m = jnp.max(work, axis=-1, keepdims=True)
        idx = jnp.min(jnp.where(work == m, lane, float(N_EXPERTS)), axis=-1, keepdims=True)
        pe = jnp.exp(m - top)
        route = route + jnp.where(col == k, pe, 0.0) + jnp.where(col == TOP_K + k, idx, 0.0)
        denom = denom + pe
        work = jnp.where(lane == idx, -jnp.inf, work)
    route_ref[...] = jnp.where(col < TOP_K, route / denom, route)


def _outproj(layer, o_mix, x, mod, mg, w_out, g2, router_w, rb):
    row = lambda i: (i, 0)
    full = lambda i: (0, 0)
    return pl.pallas_call(
        _outproj_body,
        grid=(T_ALL // TM,),
        in_specs=[pl.BlockSpec((TM, DG), row)] * 4 + [
            pl.BlockSpec((TM, D_MODEL), row),
            _mod_spec(layer),
            pl.BlockSpec((1, D_MODEL), full),
            pl.BlockSpec((None, D_MODEL, D_MODEL), lambda i: (layer, 0, 0)),
            pl.BlockSpec((1, D_MODEL), full),
            pl.BlockSpec((None, D_MODEL, N_EXPERTS), lambda i: (layer, 0, 0)),
            pl.BlockSpec((1, N_EXPERTS), full)],
        out_specs=[pl.BlockSpec((TM, D_MODEL), row), pl.BlockSpec((TM, D_MODEL), row),
                   pl.BlockSpec((TM, 2 * TOP_K), row)],
        out_shape=[jax.ShapeDtypeStruct((T_ALL, D_MODEL), F32),
                   jax.ShapeDtypeStruct((T_ALL, D_MODEL), BF16),
                   jax.ShapeDtypeStruct((T_ALL, 2 * TOP_K), F32)],
        scratch_shapes=[pltpu.VMEM((D_MODEL, D_MODEL), BF16)],
        compiler_params=_cparams(1), name="outproj_router",
    )(*o_mix, x, mod, mg.reshape(1, D_MODEL), w_out, g2.reshape(1, D_MODEL), router_w,
      rb.reshape(1, N_EXPERTS))


def _route_tables(route):
    prob = route[:, :TOP_K].reshape(N_PARTS, MOE_SLOTS)
    eid = route[:, TOP_K:].astype(jnp.int32).reshape(N_PARTS, MOE_SLOTS)
    slot = jnp.arange(MOE_SLOTS, dtype=jnp.int32)
    sslot = jnp.sort(eid * MOE_SLOTS + slot[None, :], axis=-1) % MOE_SLOTS
    experts = jnp.arange(N_EXPERTS, dtype=jnp.int32)
    cnt = jnp.sum((eid[:, :, None] == experts[None, None, :]).astype(jnp.int32), axis=1)
    zero = jnp.zeros((N_PARTS, 1), jnp.int32)
    offs = jnp.concatenate([zero, jnp.cumsum(cnt, axis=1)], axis=1)
    toffs = jnp.concatenate([zero, jnp.cumsum((cnt + MOE_TILE - 1) // MOE_TILE, axis=1)], axis=1)
    pos = jnp.arange(MOE_TILES * MOE_TILE, dtype=jnp.int32)
    e_of = jnp.sum(((pos // MOE_TILE)[None, :, None] >= toffs[:, None, 1:]).astype(jnp.int32), axis=-1)
    e_of = jnp.minimum(e_of, N_EXPERTS - 1)
    rank = pos[None, :] - jnp.take_along_axis(toffs, e_of, axis=1) * MOE_TILE
    valid = (rank < jnp.take_along_axis(cnt, e_of, axis=1)) & ((pos // MOE_TILE)[None, :] < toffs[:, -1:])
    src = jnp.clip(jnp.take_along_axis(offs, e_of, axis=1) + rank, 0, MOE_SLOTS - 1)
    ss = jnp.take_along_axis(sslot, src, axis=1)
    tok = jnp.where(valid, ss // TOP_K, -1)
    wgt = jnp.where(valid, jnp.take_along_axis(prob, ss, axis=1), 0.0)
    return (toffs.reshape(-1), tok.reshape(N_PARTS * MOE_TILES, MOE_TILE),
            wgt.reshape(N_PARTS * MOE_TILES, MOE_TILE))


def _moe_body(toffs_ref, h_ref, tok_ref, wgt_ref, wgu_ref, bgu_ref, wdn_ref, bdn_ref, o_ref):
    p, e = pl.program_id(0), pl.program_id(1)

    @pl.when(e == 0)
    def _():
        o_ref[...] = jnp.zeros((MOE_PART, D_MODEL), F32)

    ones = jnp.ones((MOE_TILE, MOE_TILE), BF16)
    eye = _iota((MOE_TILE, MOE_TILE), 0) == _iota((MOE_TILE, MOE_TILE), 1)
    lane_tok = _iota((MOE_TILE, MOE_PART), 1)
    lane_hi = (lane_tok // 64).astype(F32)
    lane_lo = (lane_tok % 64).astype(F32)
    row_tok = _iota((MOE_PART, MOE_TILE), 0)
    row_hi = (row_tok // 64).astype(F32)
    row_lo = (row_tok % 64).astype(F32)

    def to_column(row):
        return jnp.dot(jnp.where(eye, row, 0.0).astype(BF16), ones, preferred_element_type=F32)

    def tile_step(ti, carry):
        tokr = tok_ref[pl.ds(ti, 1), :].astype(F32)
        hi = jnp.floor(tokr * (1.0 / 64.0))
        lo = tokr - 64.0 * hi
        hi_c = jnp.concatenate([to_column(hi)] * (MOE_PART // MOE_TILE), axis=1)
        lo_c = jnp.concatenate([to_column(lo)] * (MOE_PART // MOE_TILE), axis=1)
        gather = ((lane_hi == hi_c) & (lane_lo == lo_c)).astype(BF16)
        xs = jnp.dot(gather, h_ref[...], preferred_element_type=F32).astype(BF16)
        gu = jnp.dot(xs, wgu_ref[...], preferred_element_type=F32) + bgu_ref[...]
        gate = jnp.minimum(gu[:, :D_FF], SWIGLU_LIMIT)
        up = jnp.clip(gu[:, D_FF:], -SWIGLU_LIMIT, SWIGLU_LIMIT)
        act = (up + 1.0) * gate * _sigmoid(SWIGLU_ALPHA * gate)
        ys = jnp.dot(act.astype(BF16), wdn_ref[...], preferred_element_type=F32) + bdn_ref[...]
        wr = wgt_ref[pl.ds(ti, 1), :]
        w_hi = wr.astype(BF16).astype(F32)
        w_col = to_column(w_hi) + to_column(wr - w_hi)
        ysw = (ys * jnp.concatenate([w_col] * (D_MODEL // MOE_TILE), axis=1)).astype(BF16)
        scatter = ((row_hi == hi) & (row_lo == lo)).astype(BF16)
        for c in range(MOE_PART // 512):
            o_ref[c * 512:(c + 1) * 512, :] += jnp.dot(scatter[c * 512:(c + 1) * 512, :], ysw,
                                                      preferred_element_type=F32)
        return carry

    lax.fori_loop(toffs_ref[p * (N_EXPERTS + 1) + e], toffs_ref[p * (N_EXPERTS + 1) + e + 1], tile_step, 0)


def _moe(layer, h2, toffs, tok, wgt, w_gu, b_gu, w_dn, b_dn):
    grid_spec = pltpu.PrefetchScalarGridSpec(
        num_scalar_prefetch=1,
        grid=(N_PARTS, N_EXPERTS),
        in_specs=[pl.BlockSpec((MOE_PART, D_MODEL), lambda p, e, t: (p, 0)),
                  pl.BlockSpec((MOE_TILES, MOE_TILE), lambda p, e, t: (p, 0)),
                  pl.BlockSpec((MOE_TILES, MOE_TILE), lambda p, e, t: (p, 0)),
                  pl.BlockSpec((None, None, D_MODEL, 2 * D_FF), lambda p, e, t: (layer, e, 0, 0)),
                  pl.BlockSpec((None, None, 1, 2 * D_FF), lambda p, e, t: (layer, e, 0, 0)),
                  pl.BlockSpec((None, None, D_FF, D_MODEL), lambda p, e, t: (layer, e, 0, 0)),
                  pl.BlockSpec((None, None, 1, D_MODEL), lambda p, e, t: (layer, e, 0, 0))],
        out_specs=pl.BlockSpec((MOE_PART, D_MODEL), lambda p, e, t: (p, 0)),
    )
    return pl.pallas_call(
        _moe_body,
        grid_spec=grid_spec,
        out_shape=jax.ShapeDtypeStruct((T_ALL, D_MODEL), F32),
        compiler_params=_cparams(2), name="moe",
    )(toffs, h2, tok, wgt, w_gu, b_gu.reshape(DEPTH, N_EXPERTS, 1, 2 * D_FF), w_dn,
      b_dn.reshape(DEPTH, N_EXPERTS, 1, D_MODEL))


def _final_body(x1_ref, m_ref, mod_ref, g_ref, o_ref):
    r = _mod_row(pl.program_id(0))
    o_ref[...] = _rms(x1_ref[...] + _mod_chunk(mod_ref, r, 5) * m_ref[...]) * g_ref[...]


def _final_norm(x1, moe_out, mod, g):
    tok_spec = pl.BlockSpec((TM, D_MODEL), lambda i: (i, 0))
    return pl.pallas_call(
        _final_body,
        grid=(T_ALL // TM,),
        in_specs=[tok_spec, tok_spec, _mod_spec(DEPTH - 1), pl.BlockSpec((1, D_MODEL), lambda i: (0, 0))],
        out_specs=tok_spec,
        out_shape=jax.ShapeDtypeStruct((T_ALL, D_MODEL), F32),
        compiler_params=_cparams(1), name="final_norm",
    )(x1, moe_out, mod, g.reshape(1, D_MODEL))


def _to_block_diag(s):
    eye = jnp.eye(4, dtype=s.dtype)
    out = jnp.einsum('hg,...hab->...hagb', eye, s)
    return out.reshape(s.shape[:-3] + (DG, DG))


def _from_block_diag(m):
    blocks = m.reshape(m.shape[:-2] + (4, HD, 4, HD))
    return jnp.stack([blocks[..., h, :, h, :] for h in range(4)], axis=-3)


def _pad_rows(w, first, total=128):
    pad = [(0, 0)] * (w.ndim - 2) + [(first, total - first - w.shape[-2]), (0, 0)]
    return jnp.pad(w, pad)


def kernel(x_prompt, x_sample, c, state_hgrn, state_lru, state_rwkv, state_s5, c_ctx, ada_w, ada_b, norm1_g, norm2_g, w_in, w_out, merge_g, hgrn_lb_logits, hgrn_norm_g, lru_conv_w, lru_conv_b, lru_wa, lru_ba, lru_wx, lru_bx, lru_lambda, rwkv_mu, rwkv_w0, rwkv_w_up, rwkv_a0, rwkv_a_up, rwkv_g_up, rwkv_k_k, rwkv_k_a, rwkv_r_k, rwkv_ln_g, rwkv_ln_b, s5_a_re, s5_a_im, s5_log_step, s5_b_re, s5_b_im, s5_c_re, s5_c_im, s5_d, s5_glu_w, s5_glu_b, router_w, router_b, moe_w_gu, moe_b_gu, moe_w_down, moe_b_down, final_norm_g):
    cvec = jnp.concatenate([c_ctx[None], c, jnp.zeros((5, D_MODEL), F32)], axis=0)
    x_cat = jnp.concatenate([x_prompt.reshape(T_CTX, D_MODEL),
                             x_sample.reshape(N_LAT_SEQ * LAT_LEN, D_MODEL)], axis=0)
    lbl = hgrn_lb_logits.reshape(DEPTH, 2 * DG)
    hg_s0 = _to_block_diag(jnp.swapaxes(state_hgrn, -1, -2))
    rw_s0 = _to_block_diag(state_rwkv)
    s5_x0 = jnp.moveaxis(state_s5.reshape(N_LAT_SEQ, DEPTH, 2, S5_N, 2), -1, -2)
    lru_pv = jnp.concatenate([lru_conv_w, lru_conv_b[:, None], lru_ba, lru_bx, lru_lambda,
                              jnp.zeros((DEPTH, 5, DG), F32)], axis=1)
    lru_wa_bd = _to_block_diag(lru_wa)
    lru_wx_bd = _to_block_diag(lru_wx)
    rw_pv = jnp.concatenate([rwkv_mu, rwkv_w0, rwkv_a0, rwkv_k_k[:, None], rwkv_k_a[:, None],
                             rwkv_r_k[:, None], rwkv_ln_g[:, None], rwkv_ln_b[:, None],
                             jnp.zeros((DEPTH, 4, DG), F32)], axis=1)
    rw_wup = _pad_rows(rwkv_w_up, 0)
    rw_aup = _pad_rows(rwkv_a_up, 32)
    rw_gup = _pad_rows(rwkv_g_up, 64)
    s5_pa = jnp.stack([s5_a_re.reshape(DEPTH, 2, S5_N), s5_a_im.reshape(DEPTH, 2, S5_N),
                       jnp.repeat(s5_log_step, 64, axis=-1)], axis=2)
    s5_pa = jnp.pad(s5_pa, ((0, 0), (0, 0), (0, 5), (0, 0)))
    s5_pv = jnp.concatenate([s5_d[:, None], s5_glu_b[:, None], jnp.zeros((DEPTH, 6, DG), F32)], axis=1)
    eye16 = jnp.eye(16, dtype=F32)
    s5_bre = jnp.einsum('gh,ldgpc->ldgchp', eye16, s5_b_re).reshape(DEPTH, 2, DG, S5_N)
    s5_bim = jnp.einsum('gh,ldgpc->ldgchp', eye16, s5_b_im).reshape(DEPTH, 2, DG, S5_N)
    s5_cre = jnp.einsum('gh,ldgcp->ldgphc', eye16, s5_c_re).reshape(DEPTH, 2, S5_N, DG)
    s5_cim = jnp.einsum('gh,ldgcp->ldgphc', eye16, s5_c_im).reshape(DEPTH, 2, S5_N, DG)

    mod = _ada_table(cvec, ada_w, ada_b)
    x1 = _embed(x_cat, _grid_pos_table())
    moe_out = None
    w_gu_bf = moe_w_gu.astype(BF16)
    w_dn_bf = moe_w_down.astype(BF16)

    new_hgrn, new_lru, new_rwkv, new_s5 = [], [], [], []
    for l in range(DEPTH):
        x, za, zb, zc, zd = _inproj(l, x1, moe_out, mod, norm1_g[l], w_in)

        hg_body = functools.partial(_hgrn_body, l)
        hg_params = (lbl, hgrn_norm_g[l].reshape(1, DG))
        hg_extra = (pltpu.VMEM((DG, DG), F32),)
        oa, sa = _mixer_call(hg_body, "hgrn", True, za, A_COLS, hg_params, None, (2, DG, DG),
                             (DG,) * 4, hg_extra, None)
        oa, _ = _mixer_call(hg_body, "hgrn", False, za, A_COLS, hg_params, hg_s0[:, l], (2, DG, DG),
                            (DG,) * 4, hg_extra, oa)

        lru_params = (lru_pv[l], lru_wa_bd[l], lru_wx_bd[l])
        ob, sb = _mixer_call(_lru_body, "lru", True, zb, B_COLS, lru_params, None, (2, DG),
                             (DG, DG, DG), (), None)
        ob, _ = _mixer_call(_lru_body, "lru", False, zb, B_COLS, lru_params, state_lru[:, l], (2, DG),
                            (DG, DG, DG), (), ob)

        rw_params = (rw_pv[l], rw_wup[l], rw_aup[l], rw_gup[l])
        rw_extra = (pltpu.VMEM((DG, DG), F32),)
        oc, sc = _mixer_call(_rwkv_body, "rwkv", True, zc, C_COLS, rw_params, None, (2, DG, DG),
                             (DG,) * RW_SCRATCH, rw_extra, None)
        oc, _ = _mixer_call(_rwkv_body, "rwkv", False, zc, C_COLS, rw_params, rw_s0[:, l], (2, DG, DG),
                            (DG,) * RW_SCRATCH, rw_extra, oc)

        s5_params = (s5_pa[l], s5_pv[l], s5_bre[l], s5_bim[l], s5_cre[l], s5_cim[l], s5_glu_w[l])
        od, sd = _mixer_call(_s5_body, "s5", True, zd, D_COLS, s5_params, None, (2, 2, S5_N),
                             (S5_N, S5_N, DG), (), None)
        od, _ = _mixer_call(_s5_body, "s5", False, zd, D_COLS, s5_params, s5_x0[:, l], (2, 2, S5_N),
                            (S5_N, S5_N, DG), (), od)

        x1, h2, route = _outproj(l, (oa, ob, oc, od), x, mod, merge_g[l], w_out, norm2_g[l],
                                 router_w, router_b[l])
        toffs, tok, wgt = _route_tables(route)
        moe_out = _moe(l, h2, toffs, tok, wgt, w_gu_bf, moe_b_gu, w_dn_bf, moe_b_down)
        new_hgrn.append(sa)
        new_lru.append(sb)
        new_rwkv.append(sc)
        new_s5.append(sd)

    y = _final_norm(x1, moe_out, mod, final_norm_g)
    y_prompt = y[:T_CTX].reshape(N_CTX_SEQ, CTX_LEN, D_MODEL)
    y_sample = y[T_CTX:].reshape(N_LAT_SEQ, LAT_LEN, D_MODEL)
    st_hgrn = jnp.swapaxes(_from_block_diag(jnp.stack(new_hgrn, axis=1)), -1, -2)
    st_lru = jnp.stack(new_lru, axis=1)
    st_rwkv = _from_block_diag(jnp.stack(new_rwkv, axis=1))
    st_s5 = jnp.moveaxis(jnp.stack(new_s5, axis=1), -2, -1).reshape(N_CTX_SEQ, DEPTH, 2, 16, 64, 2)
    return (y_prompt, y_sample, st_hgrn, st_lru, st_rwkv, st_s5)
```

```python
import functools
import math

import jax
import jax.numpy as jnp
from jax import lax
from jax.experimental import pallas as pl
from jax.experimental.pallas import tpu as pltpu

F32 = jnp.float32
BF16 = jnp.bfloat16

D_MODEL = 1024
DEPTH = 4
N_CTX_SEQ, CTX_LEN = 16, 256
N_LAT_SEQ, LAT_LEN = 2, 1024
T_CTX = N_CTX_SEQ * CTX_LEN
T_ALL = T_CTX + N_LAT_SEQ * LAT_LEN
GRID_W = 64
DG = 256
HD = 64
EPS = 1e-6
A_COLS, B_COLS, C_COLS, D_COLS = 5 * DG, 2 * DG, 3 * DG + 128, DG
N_IN = A_COLS + B_COLS + C_COLS + D_COLS
HG_CHUNK = 8
RW_CHUNK = 64
RW_GROUP = 64
SC_CHUNK = 8
LRU_C = 8.0
C_LN_EPS = 64e-5
S5_N = 1024
N_EXPERTS = 32
TOP_K = 4
D_FF = 1024
SWIGLU_ALPHA = 1.702
SWIGLU_LIMIT = 7.0
TM = 256
MOE_PART = 1024
N_PARTS = T_ALL // MOE_PART
MOE_SLOTS = MOE_PART * TOP_K
MOE_TILE = 128
MOE_TILES = MOE_SLOTS // MOE_TILE + N_EXPERTS
VMEM_LIMIT = 56 * 1024 * 1024
NEG_BIG = -1e30


def _cparams(n_axes):
    return pltpu.CompilerParams(dimension_semantics=("arbitrary",) * n_axes,
                                vmem_limit_bytes=VMEM_LIMIT)


def _iota(shape, axis):
    return lax.broadcasted_iota(jnp.int32, shape, axis)


def _shift(x, d):
    n = x.shape[0]
    d = d % n
    return x if d == 0 else pltpu.roll(x, d, 0)


def _mm(a, b):
    return jnp.dot(a.astype(BF16), b.astype(BF16), preferred_element_type=F32)


def _mm_nt(a, b):
    return lax.dot_general(a.astype(BF16), b.astype(BF16), (((1,), (1,)), ((), ())),
                           preferred_element_type=F32)


def _mm_tn(a, b):
    return lax.dot_general(a, b, (((0,), (0,)), ((), ())), preferred_element_type=F32)


def _block_ones(n, blk):
    r = _iota((n, n), 0) // blk
    c = _iota((n, n), 1) // blk
    return r == c


def _head_sum(x, ones_bf16):
    hi = x.astype(BF16)
    lo = (x - hi.astype(F32)).astype(BF16)
    return (jnp.dot(hi, ones_bf16, preferred_element_type=F32)
            + jnp.dot(lo, ones_bf16, preferred_element_type=F32))


def _sigmoid(x):
    return 1.0 / (1.0 + jnp.exp(-x))


def _silu(x):
    return x * _sigmoid(x)


def _softplus(x):
    return jnp.maximum(x, 0.0) + jnp.log(1.0 + jnp.exp(-jnp.abs(x)))


def _gelu(x):
    c = math.sqrt(2.0 / math.pi)
    return 0.5 * x * (1.0 + jnp.tanh(c * (x + 0.044715 * (x * x * x))))


def _rms(x, eps=EPS):
    return x * lax.rsqrt(jnp.mean(x * x, axis=-1, keepdims=True) + eps)


def _chunk_cumsums(x, chunk):
    j = _iota(x.shape, 0) % chunk
    f, r = x, x
    s = 1
    while s < chunk:
        f = f + jnp.where(j >= s, _shift(f, s), 0.0)
        r = r + jnp.where(j < chunk - s, _shift(r, -s), 0.0)
        s *= 2
    return f, r


def _mod_row(i):
    start = i * TM
    return jnp.where(start < T_CTX, 0, 1 + (start - T_CTX) // LAT_LEN)


def _ada_body(c_ref, w_ref, b_ref, o_ref):
    o_ref[...] = _mm(_silu(c_ref[...]), w_ref[...]) + b_ref[...]


def _ada_table(cvec, ada_w, ada_b):
    nb = 6 * D_MODEL // 1024
    return pl.pallas_call(
        _ada_body,
        grid=(DEPTH, nb),
        in_specs=[pl.BlockSpec((8, D_MODEL), lambda l, j: (0, 0)),
                  pl.BlockSpec((None, D_MODEL, 1024), lambda l, j: (l, 0, j)),
                  pl.BlockSpec((None, 1, 1024), lambda l, j: (l, 0, j))],
        out_specs=pl.BlockSpec((None, 8, 1024), lambda l, j: (l, 0, j)),
        out_shape=jax.ShapeDtypeStruct((DEPTH, 8, 6 * D_MODEL), F32),
        compiler_params=_cparams(2), name="ada_table",
    )(cvec, ada_w, ada_b.reshape(DEPTH, 1, 6 * D_MODEL))


def _embed_body(x_ref, p_ref, o_ref):
    i = pl.program_id(0)

    @pl.when(i * TM < T_CTX)
    def _():
        o_ref[...] = x_ref[...]

    @pl.when(i * TM >= T_CTX)
    def _():
        o_ref[...] = x_ref[...] + p_ref[...]


def _embed(x_cat, pos):
    per_seq = LAT_LEN // TM
    first = T_CTX // TM
    return pl.pallas_call(
        _embed_body,
        grid=(T_ALL // TM,),
        in_specs=[pl.BlockSpec((TM, D_MODEL), lambda i: (i, 0)),
                  pl.BlockSpec((TM, D_MODEL), lambda i: (jnp.maximum(i - first, 0) % per_seq, 0))],
        out_specs=pl.BlockSpec((TM, D_MODEL), lambda i: (i, 0)),
        out_shape=jax.ShapeDtypeStruct((T_ALL, D_MODEL), F32),
        compiler_params=_cparams(1), name="embed",
    )(x_cat, pos)


def _grid_pos_table():
    t = jnp.arange(LAT_LEN)
    row = (t // GRID_W).astype(F32)
    col = (t % GRID_W).astype(F32)
    quarter = D_MODEL // 4
    freq = jnp.exp(-math.log(10000.0) * jnp.arange(quarter, dtype=F32) / quarter)

    def enc(p):
        ang = p[:, None] * freq[None, :]
        return jnp.concatenate([jnp.sin(ang), jnp.cos(ang)], axis=-1)
    return jnp.concatenate([enc(row), enc(col)], axis=-1)


def _mod_chunk(mod_ref, row, n):
    return mod_ref[pl.ds(row, 1), n * D_MODEL:(n + 1) * D_MODEL]


def _mod_spec(layer):
    return pl.BlockSpec((None, 8, 6 * D_MODEL), lambda i: (layer, 0, 0))


def _inproj_body(has_moe, *refs):
    if has_moe:
        (x1_ref, m_ref, pmod_ref, mod_ref, g_ref, w_ref,
         x_ref, za_ref, zb_ref, zc_ref, zd_ref, wbf_ref) = refs
    else:
        x_ref, mod_ref, g_ref, w_ref, za_ref, zb_ref, zc_ref, zd_ref, wbf_ref = refs
    i = pl.program_id(0)

    @pl.when(i == 0)
    def _():
        wbf_ref[...] = w_ref[...].astype(BF16)

    r = _mod_row(i)
    if has_moe:
        x = x1_ref[...] + _mod_chunk(pmod_ref, r, 5) * m_ref[...]
        x_ref[...] = x
    else:
        x = x_ref[...]
    h = _rms(x) * g_ref[...] * (1.0 + _mod_chunk(mod_ref, r, 1)) + _mod_chunk(mod_ref, r, 0)
    z = jnp.dot(h.astype(BF16), wbf_ref[...], preferred_element_type=F32)
    za_ref[...] = z[:, 0:A_COLS]
    zb_ref[...] = z[:, A_COLS:A_COLS + B_COLS]
    zc_ref[...] = z[:, A_COLS + B_COLS:A_COLS + B_COLS + C_COLS]
    zd_ref[...] = z[:, A_COLS + B_COLS + C_COLS:N_IN]


def _inproj(layer, x, moe_out, mod, g, w_in):
    has_moe = moe_out is not None
    row = lambda i: (i, 0)
    widths = (A_COLS, B_COLS, C_COLS, D_COLS)
    tok_spec = pl.BlockSpec((TM, D_MODEL), row)
    in_specs = [tok_spec]
    args = [x]
    if has_moe:
        in_specs += [tok_spec, _mod_spec(layer - 1)]
        args += [moe_out, mod]
    in_specs += [_mod_spec(layer), pl.BlockSpec((1, D_MODEL), lambda i: (0, 0)),
                 pl.BlockSpec((None, D_MODEL, N_IN), lambda i: (layer, 0, 0))]
    args += [mod, g.reshape(1, D_MODEL), w_in]
    out_specs = [pl.BlockSpec((TM, wd), row) for wd in widths]
    out_shape = [jax.ShapeDtypeStruct((T_ALL, wd), F32) for wd in widths]
    if has_moe:
        out_specs = [tok_spec] + out_specs
        out_shape = [jax.ShapeDtypeStruct((T_ALL, D_MODEL), F32)] + out_shape
    outs = pl.pallas_call(
        functools.partial(_inproj_body, has_moe),
        grid=(T_ALL // TM,),
        in_specs=in_specs, out_specs=out_specs, out_shape=out_shape,
        scratch_shapes=[pltpu.VMEM((D_MODEL, N_IN), BF16)],
        compiler_params=_cparams(1), name="inproj",
    )(*args)
    return list(outs) if has_moe else [x] + list(outs)


def _hgrn_body(layer, seq_len, zero_init, *refs):
    if zero_init:
        za_ref, lbl_ref, ng_ref, o_ref, sf_ref, qs_ref, ke_ref, tot_ref, acc_ref, s_ref = refs
        s0_ref = None
    else:
        za_ref, lbl_ref, ng_ref, s0_ref, o_ref, sf_ref, qs_ref, ke_ref, tot_ref, acc_ref, s_ref = refs
    L = seq_len
    n_chunks = L // HG_CHUNK
    ones = _block_ones(DG, HD).astype(BF16)
    bd = _block_ones(DG, HD).astype(F32)

    lg = lbl_ref[...]
    e = jnp.exp(lg - jnp.max(lg, axis=0, keepdims=True))
    p = e / jnp.sum(e, axis=0, keepdims=True)
    lb = jnp.zeros((1, 2 * DG), F32)
    for m in range(1, layer + 1):
        lb = lb + p[m:m + 1, :]

    q = _silu(za_ref[:, 0:DG])
    v = za_ref[:, 3 * DG:4 * DG]
    j = _iota((L, DG), 0) % HG_CHUNK
    acc_ref[...] = jnp.zeros((L, DG), F32)

    for d in range(2):
        lbd = lb[:, d * DG:(d + 1) * DG]
        f = lbd + (1.0 - lbd) * _sigmoid(za_ref[:, (1 + d) * DG:(2 + d) * DG])
        k = 1.0 - f
        lf = jnp.log(f)
        fc, rc = _chunk_cumsums(lf, HG_CHUNK)
        lam, rest = (fc, rc - lf) if d == 0 else (rc, fc - lf)
        qs_ref[...] = q * jnp.exp(lam)
        ke_ref[...] = k * jnp.exp(rest)
        tot_ref[...] = fc + rc - lf

        o = _head_sum(q * k, ones) * v
        for lag in range(1, HG_CHUNK):
            sgn = lag if d == 0 else -lag
            valid = (j >= lag) if d == 0 else (j < HG_CHUNK - lag)
            ex = jnp.exp(jnp.where(valid, lam - _shift(lam, sgn), NEG_BIG))
            o = o + _mm(q * _shift(k, sgn) * ex, ones) * _shift(v, sgn)
        acc_ref[...] += o

        if zero_init:
            s_ref[...] = jnp.zeros((DG, DG), F32)
        else:
            s_ref[...] = s0_ref[d]

        def chunk_step(c, carry, d=d):
            ci = c if d == 0 else n_chunks - 1 - c
            r0 = pl.multiple_of(ci * HG_CHUNK, HG_CHUNK)
            s = s_ref[...]
            acc_ref[pl.ds(r0, HG_CHUNK), :] += _mm_nt(qs_ref[pl.ds(r0, HG_CHUNK), :], s)
            dec = jnp.exp(tot_ref[pl.ds(r0, 1), :])
            upd = _mm_tn(za_ref[pl.ds(r0, HG_CHUNK), 3 * DG:4 * DG], ke_ref[pl.ds(r0, HG_CHUNK), :])
            s_ref[...] = s * dec + upd * bd
            return carry
        lax.fori_loop(0, n_chunks, chunk_step, 0, unroll=4)
        sf_ref[d] = s_ref[...]

    o = acc_ref[...]
    ms = _head_sum(o * o, ones) * (1.0 / HD)
    o_ref[...] = o * lax.rsqrt(ms + EPS) * ng_ref[...] * _silu(za_ref[:, 4 * DG:5 * DG])


def _seq_specs(L, width, first_block):
    return pl.BlockSpec((L, width), lambda b: (first_block + b, 0))


def _lru_body(seq_len, zero_init, *refs):
    if zero_init:
        zb_ref, pv_ref, wa_ref, wx_ref, o_ref, hf_ref, a_ref, u_ref, acc_ref = refs
        h0_ref = None
    else:
        zb_ref, pv_ref, wa_ref, wx_ref, h0_ref, o_ref, hf_ref, a_ref, u_ref, acc_ref = refs
    L = seq_len
    n_chunks = L // SC_CHUNK
    rows = _iota((L, DG), 0)
    j = rows % SC_CHUNK
    xb = zb_ref[:, 0:DG]
    xc = (pv_ref[0:1, :] * jnp.where(rows >= 2, _shift(xb, 2), 0.0)
          + pv_ref[1:2, :] * jnp.where(rows >= 1, _shift(xb, 1), 0.0)
          + pv_ref[2:3, :] * xb
          + pv_ref[3:4, :] * jnp.where(rows < L - 1, _shift(xb, -1), 0.0)
          + pv_ref[4:5, :])
    acc_ref[...] = jnp.zeros((L, DG), F32)
    for d in range(2):
        ba, bx, lam = pv_ref[5 + d:6 + d, :], pv_ref[7 + d:8 + d, :], pv_ref[9 + d:10 + d, :]
        r = _sigmoid(_mm(xc, wa_ref[d]) + ba)
        i = _sigmoid(_mm(xc, wx_ref[d]) + bx)
        log_a = -LRU_C * r * _softplus(-lam)
        a = jnp.exp(log_a)
        u = jnp.sqrt(1.0 - jnp.exp(2.0 * log_a)) * (i * xc)
        s = 1
        while s < SC_CHUNK:
            sgn = s if d == 0 else -s
            valid = (j >= s) if d == 0 else (j < SC_CHUNK - s)
            u = u + a * jnp.where(valid, _shift(u, sgn), 0.0)
            a = a * jnp.where(valid, _shift(a, sgn), 1.0)
            s *= 2
        a_ref[...] = a
        u_ref[...] = u
        h0 = jnp.zeros((1, DG), F32) if zero_init else h0_ref[d:d + 1, :]

        def chunk_step(c, carry, d=d):
            ci = c if d == 0 else n_chunks - 1 - c
            r0 = pl.multiple_of(ci * SC_CHUNK, SC_CHUNK)
            h = u_ref[pl.ds(r0, SC_CHUNK), :] + a_ref[pl.ds(r0, SC_CHUNK), :] * carry
            acc_ref[pl.ds(r0, SC_CHUNK), :] += h
            return h[SC_CHUNK - 1:SC_CHUNK, :] if d == 0 else h[0:1, :]
        hf_ref[d:d + 1, :] = lax.fori_loop(0, n_chunks, chunk_step, h0)
    o_ref[...] = _gelu(zb_ref[:, DG:2 * DG]) * acc_ref[...]


def _mixer_call(body, name, zero_init, z, z_width, params, state0, state_shape, n_scratch_rows,
                extra_scratch, o_prev):
    L, nseq = (CTX_LEN, N_CTX_SEQ) if zero_init else (LAT_LEN, N_LAT_SEQ)
    first = 0 if zero_init else T_CTX // LAT_LEN
    in_specs = [_seq_specs(L, z_width, first)]
    args = [z]
    for p in params:
        in_specs.append(pl.BlockSpec(p.shape, lambda b, nd=p.ndim: (0,) * nd))
        args.append(p)
    nstate = len(state_shape)
    state_spec = pl.BlockSpec((None,) + state_shape, lambda b: (b,) + (0,) * nstate)
    aliases = {}
    kern = functools.partial(body, L, zero_init)
    if not zero_init:
        in_specs.append(state_spec)
        args.append(state0)
        in_specs.append(pl.BlockSpec(memory_space=pl.ANY))
        args.append(o_prev)
        aliases = {len(args) - 1: 0}
        n_in = len(args)
        inner = kern
        kern = lambda *refs: inner(*refs[:n_in - 1], *refs[n_in:])
    return pl.pallas_call(
        kern,
        grid=(nseq,),
        in_specs=in_specs,
        out_specs=[_seq_specs(L, DG, first), state_spec],
        out_shape=[jax.ShapeDtypeStruct((T_ALL, DG), F32),
                   jax.ShapeDtypeStruct((nseq,) + state_shape, F32)],
        scratch_shapes=[pltpu.VMEM((L, w), F32) for w in n_scratch_rows] + list(extra_scratch),
        input_output_aliases=aliases,
        compiler_params=_cparams(1), name=name + ("_ctx" if zero_init else "_lat"),
    )(*args)


RW_SCRATCH = 14


def _rwkv_body(seq_len, zero_init, *refs):
    if zero_init:
        zc_ref, pv_ref, wup_ref, aup_ref, gup_ref, o_ref, sf_ref = refs[:7]
        s0_ref = None
        scr = refs[7:]
    else:
        zc_ref, pv_ref, wup_ref, aup_ref, gup_ref, s0_ref, o_ref, sf_ref = refs[:8]
        scr = refs[8:]
    (kt_ref, rt_ref, kh_ref, bh_ref, kb_ref, bb_ref, tot_ref, v_ref, w_ref, u_ref, rh_ref, yh_ref,
     acc_ref, yf_ref, s_ref) = scr
    L = seq_len
    n_chunks = L // RW_CHUNK
    n_groups = L // RW_GROUP
    ones = _block_ones(DG, HD).astype(BF16)
    bd = _block_ones(DG, HD).astype(F32)
    rows = _iota((L, DG), 0)

    def pvr(i):
        return pv_ref[i:i + 1, :]

    def tshift(t, mu):
        prev = jnp.where(rows >= 1, _shift(t, 1), 0.0)
        nxt = jnp.where(rows < L - 1, _shift(t, -1), 0.0)
        return t + mu * (0.5 * (prev + nxt) - t)

    r = tshift(zc_ref[:, 0:DG], pvr(0))
    k = tshift(zc_ref[:, DG:2 * DG], pvr(1))
    v = tshift(zc_ref[:, 2 * DG:3 * DG], pvr(2))
    tail = zc_ref[:, 3 * DG:3 * DG + 128]
    kk = k * pvr(7)
    kk = kk / jnp.maximum(jnp.sqrt(_head_sum(kk * kk, ones)), 1e-12)
    gate = _mm(_sigmoid(tail), gup_ref[...])
    tanh_tail = jnp.tanh(tail)
    v_ref[...] = v
    acc_ref[...] = jnp.zeros((L, DG), F32)

    n_big = 4 * RW_GROUP
    ri, ci = _iota((n_big, n_big), 0), _iota((n_big, n_big), 1)
    same = ((ri // RW_GROUP) == (ci // RW_GROUP)) & (((ri % RW_GROUP) // RW_CHUNK) == ((ci % RW_GROUP) // RW_CHUNK))
    eye = (ri == ci).astype(F32)
    lane_head = _iota((RW_GROUP, DG), 1) // HD

    def pair_blocks(b):
        return ((ri // (2 * b)) == (ci // (2 * b))) & ((ri // b) != (ci // b))

    def stack_heads(x):
        return jnp.concatenate([jnp.where(lane_head == h, x, 0.0) for h in range(4)], axis=0)

    def fold_heads(x):
        return (x[0:RW_GROUP] + x[RW_GROUP:2 * RW_GROUP]
                + x[2 * RW_GROUP:3 * RW_GROUP] + x[3 * RW_GROUP:4 * RW_GROUP])

    for d in range(2):
        w_log = -_softplus(-(pvr(3 + d) + _mm(tanh_tail, wup_ref[d]))) - 0.5
        lw = -jnp.exp(w_log)
        a = _sigmoid(pvr(5 + d) + _mm(tail, aup_ref[d]))
        kd = k * (1.0 + (a - 1.0) * pvr(8))
        beta = kk * a
        acc_ref[...] += _head_sum(r * kd * pvr(9), ones) * v
        fc, rc = _chunk_cumsums(lw, RW_CHUNK)
        lam, rest = (fc, rc - lw) if d == 0 else (rc, fc - lw)
        kt_ref[...] = kk * jnp.exp(lam - lw)
        rt_ref[...] = r * jnp.exp(lam)
        inv = jnp.exp(-lam)
        kh_ref[...] = kd * inv
        bh_ref[...] = beta * inv
        ex_rest = jnp.exp(rest)
        kb_ref[...] = kd * ex_rest
        bb_ref[...] = beta * ex_rest
        tot_ref[...] = fc + rc - lw

        earlier = (ci < ri) if d == 0 else (ci > ri)
        m_strict = (same & earlier).astype(F32)
        m_incl = (same & (earlier | (ri == ci))).astype(F32)

        def local_step(g, carry):
            g0 = pl.multiple_of(g * RW_GROUP, RW_GROUP)
            sl = pl.ds(g0, RW_GROUP)
            lhs_k = stack_heads(kt_ref[sl, :])
            lhs_r = stack_heads(rt_ref[sl, :])
            rhs_b = jnp.concatenate([bh_ref[sl, :]] * 4, axis=0)
            rhs_k = jnp.concatenate([kh_ref[sl, :]] * 4, axis=0)
            v_big = stack_heads(v_ref[sl, :])
            n_mat = m_strict * _mm_nt(lhs_k, rhs_b)
            m_mat = m_strict * _mm_nt(lhs_k, rhs_k)
            a_rb = m_incl * _mm_nt(lhs_r, rhs_b)
            a_rk = m_incl * _mm_nt(lhs_r, rhs_k)
            t_mat = eye - jnp.where(pair_blocks(1), n_mat, 0.0)
            b = 2
            while b < RW_CHUNK:
                t_mat = t_mat - _mm(t_mat, _mm(jnp.where(pair_blocks(b), n_mat, 0.0), t_mat))
                b *= 2
            w_big = _mm(t_mat, lhs_k)
            u_big = _mm(t_mat, _mm(m_mat, v_big))
            w_ref[sl, :] = fold_heads(w_big)
            u_ref[sl, :] = fold_heads(u_big)
            rh_ref[sl, :] = fold_heads(lhs_r - _mm(a_rb, w_big))
            yh_ref[sl, :] = fold_heads(_mm(a_rk, v_big) - _mm(a_rb, u_big))
            return carry
        lax.fori_loop(0, n_groups, local_step, 0, unroll=2)

        if zero_init:
            s_ref[...] = jnp.zeros((DG, DG), F32)
        else:
            s_ref[...] = s0_ref[d]

        def chunk_step(c, carry, d=d):
            cidx = c if d == 0 else n_chunks - 1 - c
            r0 = pl.multiple_of(cidx * RW_CHUNK, RW_CHUNK)
            sl = pl.ds(r0, RW_CHUNK)
            s = s_ref[...]
            yh_ref[sl, :] = _mm_nt(rh_ref[sl, :], s) + yh_ref[sl, :]
            bb = bb_ref[sl, :].astype(BF16)
            phi = _mm_tn(w_ref[sl, :].astype(BF16), bb)
            delta = _mm_tn(jnp.concatenate([v_ref[sl, :], u_ref[sl, :]], axis=0).astype(BF16),
                           jnp.concatenate([kb_ref[sl, :].astype(BF16), -bb], axis=0))
            s_ref[...] = s * jnp.exp(tot_ref[pl.ds(r0, 1), :]) + bd * (delta - _mm(s, phi))
            return carry
        lax.fori_loop(0, n_chunks, chunk_step, 0, unroll=2)
        sf_ref[d] = s_ref[...]
        if d == 0:
            yf_ref[...] = yh_ref[...]
    y = yf_ref[...] + yh_ref[...]
    mu = _head_sum(y, ones) * (1.0 / HD)
    yc = y - mu
    var = _head_sum(yc * yc, ones) * (1.0 / HD)
    y = yc * lax.rsqrt(var + C_LN_EPS) * pvr(10) + pvr(11)
    o_ref[...] = (y + acc_ref[...]) * gate


S5_BLK = 256


def _s5_body(seq_len, zero_init, *refs):
    if zero_init:
        zd_ref, pa_ref, pv_ref, bre_ref, bim_ref, cre_ref, cim_ref, glu_ref, o_ref, xf_ref = refs[:10]
        x0_ref = None
        scr = refs[10:]
    else:
        (zd_ref, pa_ref, pv_ref, bre_ref, bim_ref, cre_ref, cim_ref, glu_ref, x0_ref,
         o_ref, xf_ref) = refs[:11]
        scr = refs[11:]
    xr_ref, xi_ref, acc_ref = scr
    L = seq_len
    n_chunks = L // SC_CHUNK
    n_blk = L // S5_BLK
    jb = _iota((S5_BLK, S5_N), 0) % SC_CHUNK
    j8 = _iota((SC_CHUNK, S5_N), 0)

    def cmul(ar, ai, br, bi):
        return ar * br - ai * bi, ar * bi + ai * br

    for d in range(2):
        a_re, a_im, ls = pa_ref[d, 0:1, :], pa_ref[d, 1:2, :], pa_ref[d, 2:3, :]
        step = jnp.exp(ls)
        mag = jnp.exp(step * a_re)
        p1r, p1i = mag * jnp.cos(step * a_im), mag * jnp.sin(step * a_im)
        den = a_re * a_re + a_im * a_im
        zr = ((p1r - 1.0) * a_re + p1i * a_im) / den
        zi = (p1i * a_re - (p1r - 1.0) * a_im) / den
        p2r, p2i = cmul(p1r, p1i, p1r, p1i)
        p4r, p4i = cmul(p2r, p2i, p2r, p2i)
        powers = ((1, p1r, p1i), (2, p2r, p2i), (4, p4r, p4i))
        pwr = jnp.broadcast_to(p1r, (SC_CHUNK, S5_N))
        pwi = jnp.broadcast_to(p1i, (SC_CHUNK, S5_N))
        for s, _, _ in powers:
            sgn = s if d == 0 else -s
            valid = (j8 >= s) if d == 0 else (j8 < SC_CHUNK - s)
            nr, ni = cmul(pwr, pwi, _shift(pwr, sgn), _shift(pwi, sgn))
            pwr, pwi = jnp.where(valid, nr, pwr), jnp.where(valid, ni, pwi)

        def blk_step(bi, carry, d=d, zr=zr, zi=zi, powers=powers):
            sl = pl.ds(pl.multiple_of(bi * S5_BLK, S5_BLK), S5_BLK)
            u = zd_ref[sl, :]
            m1 = _mm(u, bre_ref[d])
            m2 = _mm(u, bim_ref[d])
            xr = zr * m1 - zi * m2
            xi = zr * m2 + zi * m1
            for s, pr, pi in powers:
                sgn = s if d == 0 else -s
                valid = (jb >= s) if d == 0 else (jb < SC_CHUNK - s)
                sr = jnp.where(valid, _shift(xr, sgn), 0.0)
                si = jnp.where(valid, _shift(xi, sgn), 0.0)
                xr, xi = xr + pr * sr - pi * si, xi + pr * si + pi * sr
            xr_ref[sl, :] = xr
            xi_ref[sl, :] = xi
            return carry
        lax.fori_loop(0, n_blk, blk_step, 0)

        if zero_init:
            c0 = (jnp.zeros((1, S5_N), F32), jnp.zeros((1, S5_N), F32))
        else:
            c0 = (x0_ref[d, 0:1, :], x0_ref[d, 1:2, :])

        def chunk_step(c, carry, d=d, pwr=pwr, pwi=pwi):
            ci = c if d == 0 else n_chunks - 1 - c
            sl = pl.ds(pl.multiple_of(ci * SC_CHUNK, SC_CHUNK), SC_CHUNK)
            cr, cim = carry
            hr = xr_ref[sl, :] + pwr * cr - pwi * cim
            hi = xi_ref[sl, :] + pwr * cim + pwi * cr
            xr_ref[sl, :] = hr
            xi_ref[sl, :] = hi
            last = SC_CHUNK - 1 if d == 0 else 0
            return hr[last:last + 1, :], hi[last:last + 1, :]
        fr, fi = lax.fori_loop(0, n_chunks, chunk_step, c0)
        xf_ref[d, 0:1, :] = fr
        xf_ref[d, 1:2, :] = fi

        def out_step(bi, carry, d=d):
            sl = pl.ds(pl.multiple_of(bi * S5_BLK, S5_BLK), S5_BLK)
            y = _mm(xr_ref[sl, :], cre_ref[d]) - _mm(xi_ref[sl, :], cim_ref[d])
            if d == 0:
                acc_ref[sl, :] = y
            else:
                acc_ref[sl, :] += y
            return carry
        lax.fori_loop(0, n_blk, out_step, 0)

    y = _gelu(acc_ref[...] + pv_ref[0:1, :] * zd_ref[...])
    o_ref[...] = y * _sigmoid(_mm(y, glu_ref[...]) + pv_ref[1:2, :])


def _outproj_body(oa_ref, ob_ref, oc_ref, od_ref, x_ref, mod_ref, mg_ref, w_ref, g2_ref, rw_ref, rb_ref,
                  x1_ref, h2_ref, route_ref, wbf_ref):
    i = pl.program_id(0)

    @pl.when(i == 0)
    def _():
        wbf_ref[...] = w_ref[...].astype(BF16)

    r = _mod_row(i)
    y = jnp.zeros((TM, D_MODEL), F32)
    for n, ref in enumerate((oa_ref, ob_ref, oc_ref, od_ref)):
        on = _rms(ref[...]) * mg_ref[:, n * DG:(n + 1) * DG]
        y = y + jnp.dot(on.astype(BF16), wbf_ref[n * DG:(n + 1) * DG, :], preferred_element_type=F32)
    x1 = x_ref[...] + _mod_chunk(mod_ref, r, 2) * y
    x1_ref[...] = x1
    h2 = _rms(x1) * g2_ref[...] * (1.0 + _mod_chunk(mod_ref, r, 4)) + _mod_chunk(mod_ref, r, 3)
    h2_ref[...] = h2.astype(BF16)
    logits = jnp.dot(h2, rw_ref[...], precision=lax.Precision.HIGHEST,
                     preferred_element_type=F32) + rb_ref[...]
    lane = _iota((TM, N_EXPERTS), 1).astype(F32)
    col = _iota((TM, 2 * TOP_K), 1)
    work = logits
    top = jnp.max(work, axis=-1, keepdims=True)
    route = jnp.zeros((TM, 2 * TOP_K), F32)
    denom = jnp.zeros((TM, 1), F32)
    for k in range(TOP_K):
        m = jnp.max(work, axis=-1, keepdims=True)
        idx = jnp.min(jnp.where(work == m, lane, float(N_EXPERTS)), axis=-1, keepdims=True)
        pe = jnp.exp(m - top)
        route = route + jnp.where(col == k, pe, 0.0) + jnp.where(col == TOP_K + k, idx, 0.0)
        denom = denom + pe
        work = jnp.where(lane == idx, -jnp.inf, work)
    route_ref[...] = jnp.where(col < TOP_K, route / denom, route)


def _outproj(layer, o_mix, x, mod, mg, w_out, g2, router_w, rb):
    row = lambda i: (i, 0)
    full = lambda i: (0, 0)
    return pl.pallas_call(
        _outproj_body,
        grid=(T_ALL // TM,),
        in_specs=[pl.BlockSpec((TM, DG), row)] * 4 + [
            pl.BlockSpec((TM, D_MODEL), row),
            _mod_spec(layer),
            pl.BlockSpec((1, D_MODEL), full),
            pl.BlockSpec((None, D_MODEL, D_MODEL), lambda i: (layer, 0, 0)),
            pl.BlockSpec((1, D_MODEL), full),
            pl.BlockSpec((None, D_MODEL, N_EXPERTS), lambda i: (layer, 0, 0)),
            pl.BlockSpec((1, N_EXPERTS), full)],
        out_specs=[pl.BlockSpec((TM, D_MODEL), row), pl.BlockSpec((TM, D_MODEL), row),
                   pl.BlockSpec((TM, 2 * TOP_K), row)],
        out_shape=[jax.ShapeDtypeStruct((T_ALL, D_MODEL), F32),
                   jax.ShapeDtypeStruct((T_ALL, D_MODEL), BF16),
                   jax.ShapeDtypeStruct((T_ALL, 2 * TOP_K), F32)],
        scratch_shapes=[pltpu.VMEM((D_MODEL, D_MODEL), BF16)],
        compiler_params=_cparams(1), name="outproj_router",
    )(*o_mix, x, mod, mg.reshape(1, D_MODEL), w_out, g2.reshape(1, D_MODEL), router_w,
      rb.reshape(1, N_EXPERTS))


def _route_tables(route):
    prob = route[:, :TOP_K].reshape(N_PARTS, MOE_SLOTS)
    eid = route[:, TOP_K:].astype(jnp.int32).reshape(N_PARTS, MOE_SLOTS)
    slot = jnp.arange(MOE_SLOTS, dtype=jnp.int32)
    experts = jnp.arange(N_EXPERTS, dtype=jnp.int32)
    cnt = jnp.sum((eid[:, :, None] == experts[None, None, :]).astype(jnp.int32), axis=1)
    zero = jnp.zeros((N_PARTS, 1), jnp.int32)
    toffs = jnp.concatenate([zero, jnp.cumsum((cnt + MOE_TILE - 1) // MOE_TILE, axis=1)], axis=1)
    stride = MOE_SLOTS + MOE_TILE
    last = N_EXPERTS * stride
    n_dummy = MOE_TILE - 1
    dummy_j = jnp.arange(n_dummy, dtype=jnp.int32)
    need = (-cnt) % MOE_TILE
    dummy_key = jnp.where(dummy_j[None, None, :] < need[:, :, None],
                          experts[None, :, None] * stride + MOE_SLOTS + dummy_j[None, None, :], last)
    n_fill = MOE_TILES * MOE_TILE - MOE_SLOTS - N_EXPERTS * n_dummy
    keys = jnp.concatenate([eid * stride + slot[None, :], dummy_key.reshape(N_PARTS, N_EXPERTS * n_dummy),
                            jnp.full((N_PARTS, n_fill), last, jnp.int32)], axis=1)
    n_pad = MOE_TILES * MOE_TILE - MOE_SLOTS
    tok = jnp.concatenate([jnp.broadcast_to(slot // TOP_K, (N_PARTS, MOE_SLOTS)),
                           jnp.full((N_PARTS, n_pad), -1, jnp.int32)], axis=1)
    wgt = jnp.concatenate([prob, jnp.zeros((N_PARTS, n_pad), F32)], axis=1)
    _, tok, wgt = lax.sort((keys, tok, wgt), dimension=1, num_keys=1)
    return (toffs.reshape(-1), tok.reshape(N_PARTS * MOE_TILES, MOE_TILE),
            wgt.reshape(N_PARTS * MOE_TILES, MOE_TILE))


def _moe_body(toffs_ref, h_ref, tok_ref, wgt_ref, wgu_ref, bgu_ref, wdn_ref, bdn_ref, o_ref):
    p, e = pl.program_id(0), pl.program_id(1)

    @pl.when(e == 0)
    def _():
        o_ref[...] = jnp.zeros((MOE_PART, D_MODEL), F32)

    ones = jnp.ones((MOE_TILE, MOE_TILE), BF16)
    eye = _iota((MOE_TILE, MOE_TILE), 0) == _iota((MOE_TILE, MOE_TILE), 1)
    lane_tok = _iota((MOE_TILE, MOE_PART), 1)
    lane_hi = (lane_tok // 64).astype(F32)
    lane_lo = (lane_tok % 64).astype(F32)
    row_tok = _iota((MOE_PART, MOE_TILE), 0)
    row_hi = (row_tok // 64).astype(F32)
    row_lo = (row_tok % 64).astype(F32)

    def to_column(row):
        return jnp.dot(jnp.where(eye, row, 0.0).astype(BF16), ones, preferred_element_type=F32)

    def tile_step(ti, carry):
        tokr = tok_ref[pl.ds(ti, 1), :].astype(F32)
        hi = jnp.floor(tokr * (1.0 / 64.0))
        lo = tokr - 64.0 * hi
        hi_c = jnp.concatenate([to_column(hi)] * (MOE_PART // MOE_TILE), axis=1)
        lo_c = jnp.concatenate([to_column(lo)] * (MOE_PART // MOE_TILE), axis=1)
        gather = ((lane_hi == hi_c) & (lane_lo == lo_c)).astype(BF16)
        xs = jnp.dot(gather, h_ref[...], preferred_element_type=F32).astype(BF16)
        gu = jnp.dot(xs, wgu_ref[...], preferred_element_type=F32) + bgu_ref[...]
        gate = jnp.minimum(gu[:, :D_FF], SWIGLU_LIMIT)
        up = jnp.clip(gu[:, D_FF:], -SWIGLU_LIMIT, SWIGLU_LIMIT)
        act = (up + 1.0) * gate * _sigmoid(SWIGLU_ALPHA * gate)
        ys = jnp.dot(act.astype(BF16), wdn_ref[...], preferred_element_type=F32) + bdn_ref[...]
        wr = wgt_ref[pl.ds(ti, 1), :]
        w_hi = wr.astype(BF16).astype(F32)
        w_col = to_column(w_hi) + to_column(wr - w_hi)
        ysw = (ys * jnp.concatenate([w_col] * (D_MODEL // MOE_TILE), axis=1)).astype(BF16)
        scatter = ((row_hi == hi) & (row_lo == lo)).astype(BF16)
        for c in range(MOE_PART // 512):
            o_ref[c * 512:(c + 1) * 512, :] += jnp.dot(scatter[c * 512:(c + 1) * 512, :], ysw,
                                                      preferred_element_type=F32)
        return carry

    lax.fori_loop(toffs_ref[p * (N_EXPERTS + 1) + e], toffs_ref[p * (N_EXPERTS + 1) + e + 1], tile_step, 0)


def _moe(layer, h2, toffs, tok, wgt, w_gu, b_gu, w_dn, b_dn):
    grid_spec = pltpu.PrefetchScalarGridSpec(
        num_scalar_prefetch=1,
        grid=(N_PARTS, N_EXPERTS),
        in_specs=[pl.BlockSpec((MOE_PART, D_MODEL), lambda p, e, t: (p, 0)),
                  pl.BlockSpec((MOE_TILES, MOE_TILE), lambda p, e, t: (p, 0)),
                  pl.BlockSpec((MOE_TILES, MOE_TILE), lambda p, e, t: (p, 0)),
                  pl.BlockSpec((None, None, D_MODEL, 2 * D_FF), lambda p, e, t: (layer, e, 0, 0)),
                  pl.BlockSpec((None, None, 1, 2 * D_FF), lambda p, e, t: (layer, e, 0, 0)),
                  pl.BlockSpec((None, None, D_FF, D_MODEL), lambda p, e, t: (layer, e, 0, 0)),
                  pl.BlockSpec((None, None, 1, D_MODEL), lambda p, e, t: (layer, e, 0, 0))],
        out_specs=pl.BlockSpec((MOE_PART, D_MODEL), lambda p, e, t: (p, 0)),
    )
    return pl.pallas_call(
        _moe_body,
        grid_spec=grid_spec,
        out_shape=jax.ShapeDtypeStruct((T_ALL, D_MODEL), F32),
        compiler_params=_cparams(2), name="moe",
    )(toffs, h2, tok, wgt, w_gu, b_gu.reshape(DEPTH, N_EXPERTS, 1, 2 * D_FF), w_dn,
      b_dn.reshape(DEPTH, N_EXPERTS, 1, D_MODEL))


def _final_body(x1_ref, m_ref, mod_ref, g_ref, o_ref):
    r = _mod_row(pl.program_id(0))
    o_ref[...] = _rms(x1_ref[...] + _mod_chunk(mod_ref, r, 5) * m_ref[...]) * g_ref[...]


def _final_norm(x1, moe_out, mod, g):
    tok_spec = pl.BlockSpec((TM, D_MODEL), lambda i: (i, 0))
    return pl.pallas_call(
        _final_body,
        grid=(T_ALL // TM,),
        in_specs=[tok_spec, tok_spec, _mod_spec(DEPTH - 1), pl.BlockSpec((1, D_MODEL), lambda i: (0, 0))],
        out_specs=tok_spec,
        out_shape=jax.ShapeDtypeStruct((T_ALL, D_MODEL), F32),
        compiler_params=_cparams(1), name="final_norm",
    )(x1, moe_out, mod, g.reshape(1, D_MODEL))


def _to_block_diag(s):
    eye = jnp.eye(4, dtype=s.dtype)
    out = jnp.einsum('hg,...hab->...hagb', eye, s)
    return out.reshape(s.shape[:-3] + (DG, DG))


def _from_block_diag(m):
    blocks = m.reshape(m.shape[:-2] + (4, HD, 4, HD))
    return jnp.stack([blocks[..., h, :, h, :] for h in range(4)], axis=-3)


def _pad_rows(w, first, total=128):
    pad = [(0, 0)] * (w.ndim - 2) + [(first, total - first - w.shape[-2]), (0, 0)]
    return jnp.pad(w, pad)


def kernel(x_prompt, x_sample, c, state_hgrn, state_lru, state_rwkv, state_s5, c_ctx, ada_w, ada_b, norm1_g, norm2_g, w_in, w_out, merge_g, hgrn_lb_logits, hgrn_norm_g, lru_conv_w, lru_conv_b, lru_wa, lru_ba, lru_wx, lru_bx, lru_lambda, rwkv_mu, rwkv_w0, rwkv_w_up, rwkv_a0, rwkv_a_up, rwkv_g_up, rwkv_k_k, rwkv_k_a, rwkv_r_k, rwkv_ln_g, rwkv_ln_b, s5_a_re, s5_a_im, s5_log_step, s5_b_re, s5_b_im, s5_c_re, s5_c_im, s5_d, s5_glu_w, s5_glu_b, router_w, router_b, moe_w_gu, moe_b_gu, moe_w_down, moe_b_down, final_norm_g):
    cvec = jnp.concatenate([c_ctx[None], c, jnp.zeros((5, D_MODEL), F32)], axis=0)
    x_cat = jnp.concatenate([x_prompt.reshape(T_CTX, D_MODEL),
                             x_sample.reshape(N_LAT_SEQ * LAT_LEN, D_MODEL)], axis=0)
    lbl = hgrn_lb_logits.reshape(DEPTH, 2 * DG)
    hg_s0 = _to_block_diag(jnp.swapaxes(state_hgrn, -1, -2))
    rw_s0 = _to_block_diag(state_rwkv)
    s5_x0 = jnp.moveaxis(state_s5.reshape(N_LAT_SEQ, DEPTH, 2, S5_N, 2), -1, -2)
    lru_pv = jnp.concatenate([lru_conv_w, lru_conv_b[:, None], lru_ba, lru_bx, lru_lambda,
                              jnp.zeros((DEPTH, 5, DG), F32)], axis=1)
    lru_wa_bd = _to_block_diag(lru_wa)
    lru_wx_bd = _to_block_diag(lru_wx)
    rw_pv = jnp.concatenate([rwkv_mu, rwkv_w0, rwkv_a0, rwkv_k_k[:, None], rwkv_k_a[:, None],
                             rwkv_r_k[:, None], rwkv_ln_g[:, None], rwkv_ln_b[:, None],
                             jnp.zeros((DEPTH, 4, DG), F32)], axis=1)
    rw_wup = _pad_rows(rwkv_w_up, 0)
    rw_aup = _pad_rows(rwkv_a_up, 32)
    rw_gup = _pad_rows(rwkv_g_up, 64)
    s5_pa = jnp.stack([s5_a_re.reshape(DEPTH, 2, S5_N), s5_a_im.reshape(DEPTH, 2, S5_N),
                       jnp.repeat(s5_log_step, 64, axis=-1)], axis=2)
    s5_pa = jnp.pad(s5_pa, ((0, 0), (0, 0), (0, 5), (0, 0)))
    s5_pv = jnp.concatenate([s5_d[:, None], s5_glu_b[:, None], jnp.zeros((DEPTH, 6, DG), F32)], axis=1)
    eye16 = jnp.eye(16, dtype=F32)
    s5_bre = jnp.einsum('gh,ldgpc->ldgchp', eye16, s5_b_re).reshape(DEPTH, 2, DG, S5_N)
    s5_bim = jnp.einsum('gh,ldgpc->ldgchp', eye16, s5_b_im).reshape(DEPTH, 2, DG, S5_N)
    s5_cre = jnp.einsum('gh,ldgcp->ldgphc', eye16, s5_c_re).reshape(DEPTH, 2, S5_N, DG)
    s5_cim = jnp.einsum('gh,ldgcp->ldgphc', eye16, s5_c_im).reshape(DEPTH, 2, S5_N, DG)

    mod = _ada_table(cvec, ada_w, ada_b)
    x1 = _embed(x_cat, _grid_pos_table())
    moe_out = None
    w_gu_bf = moe_w_gu.astype(BF16)
    w_dn_bf = moe_w_down.astype(BF16)

    new_hgrn, new_lru, new_rwkv, new_s5 = [], [], [], []
    for l in range(DEPTH):
        x, za, zb, zc, zd = _inproj(l, x1, moe_out, mod, norm1_g[l], w_in)

        hg_body = functools.partial(_hgrn_body, l)
        hg_params = (lbl, hgrn_norm_g[l].reshape(1, DG))
        hg_extra = (pltpu.VMEM((DG, DG), F32),)
        oa, sa = _mixer_call(hg_body, "hgrn", True, za, A_COLS, hg_params, None, (2, DG, DG),
                             (DG,) * 4, hg_extra, None)
        oa, _ = _mixer_call(hg_body, "hgrn", False, za, A_COLS, hg_params, hg_s0[:, l], (2, DG, DG),
                            (DG,) * 4, hg_extra, oa)

        lru_params = (lru_pv[l], lru_wa_bd[l], lru_wx_bd[l])
        ob, sb = _mixer_call(_lru_body, "lru", True, zb, B_COLS, lru_params, None, (2, DG),
                             (DG, DG, DG), (), None)
        ob, _ = _mixer_call(_lru_body, "lru", False, zb, B_COLS, lru_params, state_lru[:, l], (2, DG),
                            (DG, DG, DG), (), ob)

        rw_params = (rw_pv[l], rw_wup[l], rw_aup[l], rw_gup[l])
        rw_extra = (pltpu.VMEM((DG, DG), F32),)
        oc, sc = _mixer_call(_rwkv_body, "rwkv", True, zc, C_COLS, rw_params, None, (2, DG, DG),
                             (DG,) * RW_SCRATCH, rw_extra, None)
        oc, _ = _mixer_call(_rwkv_body, "rwkv", False, zc, C_COLS, rw_params, rw_s0[:, l], (2, DG, DG),
                            (DG,) * RW_SCRATCH, rw_extra, oc)

        s5_params = (s5_pa[l], s5_pv[l], s5_bre[l], s5_bim[l], s5_cre[l], s5_cim[l], s5_glu_w[l])
        od, sd = _mixer_call(_s5_body, "s5", True, zd, D_COLS, s5_params, None, (2, 2, S5_N),
                             (S5_N, S5_N, DG), (), None)
        od, _ = _mixer_call(_s5_body, "s5", False, zd, D_COLS, s5_params, s5_x0[:, l], (2, 2, S5_N),
                            (S5_N, S5_N, DG), (), od)

        x1, h2, route = _outproj(l, (oa, ob, oc, od), x, mod, merge_g[l], w_out, norm2_g[l],
                                 router_w, router_b[l])
        toffs, tok, wgt = _route_tables(route)
        moe_out = _moe(l, h2, toffs, tok, wgt, w_gu_bf, moe_b_gu, w_dn_bf, moe_b_down)
        new_hgrn.append(sa)
        new_lru.append(sb)
        new_rwkv.append(sc)
        new_s5.append(sd)

    y = _final_norm(x1, moe_out, mod, final_norm_g)
    y_prompt = y[:T_CTX].reshape(N_CTX_SEQ, CTX_LEN, D_MODEL)
    y_sample = y[T_CTX:].reshape(N_LAT_SEQ, LAT_LEN, D_MODEL)
    st_hgrn = jnp.swapaxes(_from_block_diag(jnp.stack(new_hgrn, axis=1)), -1, -2)
    st_lru = jnp.stack(new_lru, axis=1)
    st_rwkv = _from_block_diag(jnp.stack(new_rwkv, axis=1))
    st_s5 = jnp.moveaxis(jnp.stack(new_s5, axis=1), -2, -1).reshape(N_CTX_SEQ, DEPTH, 2, 16, 64, 2)
    return (y_prompt, y_sample, st_hgrn, st_lru, st_rwkv, st_s5)
```

```python
import functools
import math

import jax
import jax.numpy as jnp
from jax import lax
from jax.experimental import pallas as pl
from jax.experimental.pallas import tpu as pltpu

F32 = jnp.float32
BF16 = jnp.bfloat16

D_MODEL = 1024
DEPTH = 4
N_CTX_SEQ, CTX_LEN = 16, 256
N_LAT_SEQ, LAT_LEN = 2, 1024
T_CTX = N_CTX_SEQ * CTX_LEN
T_ALL = T_CTX + N_LAT_SEQ * LAT_LEN
GRID_W = 64
DG = 256
HD = 64
EPS = 1e-6
A_COLS, B_COLS, C_COLS, D_COLS = 5 * DG, 2 * DG, 3 * DG + 128, DG
N_IN = A_COLS + B_COLS + C_COLS + D_COLS
HG_CHUNK = 8
RW_CHUNK = 64
RW_GROUP = 64
SC_CHUNK = 8
LRU_C = 8.0
C_LN_EPS = 64e-5
S5_N = 1024
N_EXPERTS = 32
TOP_K = 4
D_FF = 1024
SWIGLU_ALPHA = 1.702
SWIGLU_LIMIT = 7.0
TM = 256
MOE_PART = 1536
N_PARTS = T_ALL // MOE_PART
MOE_SLOTS = MOE_PART * TOP_K
MOE_TILE = 256
MOE_TILES = MOE_SLOTS // MOE_TILE + N_EXPERTS
VMEM_LIMIT = 56 * 1024 * 1024
NEG_BIG = -1e30


def _cparams(n_axes):
    return pltpu.CompilerParams(dimension_semantics=("arbitrary",) * n_axes,
                                vmem_limit_bytes=VMEM_LIMIT)


def _iota(shape, axis):
    return lax.broadcasted_iota(jnp.int32, shape, axis)


def _shift(x, d):
    n = x.shape[0]
    d = d % n
    return x if d == 0 else pltpu.roll(x, d, 0)


def _mm(a, b):
    return jnp.dot(a.astype(BF16), b.astype(BF16), preferred_element_type=F32)


def _mm_nt(a, b):
    return lax.dot_general(a.astype(BF16), b.astype(BF16), (((1,), (1,)), ((), ())),
                           preferred_element_type=F32)


def _mm_tn(a, b):
    return lax.dot_general(a, b, (((0,), (0,)), ((), ())), preferred_element_type=F32)


def _block_ones(n, blk):
    r = _iota((n, n), 0) // blk
    c = _iota((n, n), 1) // blk
    return r == c


def _head_sum(x, ones_bf16):
    hi = x.astype(BF16)
    lo = (x - hi.astype(F32)).astype(BF16)
    return (jnp.dot(hi, ones_bf16, preferred_element_type=F32)
            + jnp.dot(lo, ones_bf16, preferred_element_type=F32))


def _sigmoid(x):
    return 1.0 / (1.0 + jnp.exp(-x))


def _silu(x):
    return x * _sigmoid(x)


def _softplus(x):
    return jnp.maximum(x, 0.0) + jnp.log(1.0 + jnp.exp(-jnp.abs(x)))


def _gelu(x):
    c = math.sqrt(2.0 / math.pi)
    return 0.5 * x * (1.0 + jnp.tanh(c * (x + 0.044715 * (x * x * x))))


def _rms(x, eps=EPS):
    return x * lax.rsqrt(jnp.mean(x * x, axis=-1, keepdims=True) + eps)


def _chunk_cumsums(x, chunk):
    j = _iota(x.shape, 0) % chunk
    f, r = x, x
    s = 1
    while s < chunk:
        f = f + jnp.where(j >= s, _shift(f, s), 0.0)
        r = r + jnp.where(j < chunk - s, _shift(r, -s), 0.0)
        s *= 2
    return f, r


def _mod_row(i):
    start = i * TM
    return jnp.where(start < T_CTX, 0, 1 + (start - T_CTX) // LAT_LEN)


def _ada_body(c_ref, w_ref, b_ref, o_ref):
    o_ref[...] = _mm(_silu(c_ref[...]), w_ref[...]) + b_ref[...]


def _ada_table(cvec, ada_w, ada_b):
    nb = 6 * D_MODEL // 1024
    return pl.pallas_call(
        _ada_body,
        grid=(DEPTH, nb),
        in_specs=[pl.BlockSpec((8, D_MODEL), lambda l, j: (0, 0)),
                  pl.BlockSpec((None, D_MODEL, 1024), lambda l, j: (l, 0, j)),
                  pl.BlockSpec((None, 1, 1024), lambda l, j: (l, 0, j))],
        out_specs=pl.BlockSpec((None, 8, 1024), lambda l, j: (l, 0, j)),
        out_shape=jax.ShapeDtypeStruct((DEPTH, 8, 6 * D_MODEL), F32),
        compiler_params=_cparams(2), name="ada_table",
    )(cvec, ada_w, ada_b.reshape(DEPTH, 1, 6 * D_MODEL))


def _embed_body(x_ref, p_ref, o_ref):
    i = pl.program_id(0)

    @pl.when(i * TM < T_CTX)
    def _():
        o_ref[...] = x_ref[...]

    @pl.when(i * TM >= T_CTX)
    def _():
        o_ref[...] = x_ref[...] + p_ref[...]


def _embed(x_cat, pos):
    per_seq = LAT_LEN // TM
    first = T_CTX // TM
    return pl.pallas_call(
        _embed_body,
        grid=(T_ALL // TM,),
        in_specs=[pl.BlockSpec((TM, D_MODEL), lambda i: (i, 0)),
                  pl.BlockSpec((TM, D_MODEL), lambda i: (jnp.maximum(i - first, 0) % per_seq, 0))],
        out_specs=pl.BlockSpec((TM, D_MODEL), lambda i: (i, 0)),
        out_shape=jax.ShapeDtypeStruct((T_ALL, D_MODEL), F32),
        compiler_params=_cparams(1), name="embed",
    )(x_cat, pos)


def _grid_pos_table():
    t = jnp.arange(LAT_LEN)
    row = (t // GRID_W).astype(F32)
    col = (t % GRID_W).astype(F32)
    quarter = D_MODEL // 4
    freq = jnp.exp(-math.log(10000.0) * jnp.arange(quarter, dtype=F32) / quarter)

    def enc(p):
        ang = p[:, None] * freq[None, :]
        return jnp.concatenate([jnp.sin(ang), jnp.cos(ang)], axis=-1)
    return jnp.concatenate([enc(row), enc(col)], axis=-1)


def _mod_chunk(mod_ref, row, n):
    return mod_ref[pl.ds(row, 1), n * D_MODEL:(n + 1) * D_MODEL]


def _mod_spec(layer):
    return pl.BlockSpec((None, 8, 6 * D_MODEL), lambda i: (layer, 0, 0))


def _inproj_body(has_moe, *refs):
    if has_moe:
        (x1_ref, m_ref, pmod_ref, mod_ref, g_ref, w_ref,
         x_ref, za_ref, zb_ref, zc_ref, zd_ref, wbf_ref) = refs
    else:
        x_ref, mod_ref, g_ref, w_ref, za_ref, zb_ref, zc_ref, zd_ref, wbf_ref = refs
    i = pl.program_id(0)

    @pl.when(i == 0)
    def _():
        wbf_ref[...] = w_ref[...].astype(BF16)

    r = _mod_row(i)
    if has_moe:
        x = x1_ref[...] + _mod_chunk(pmod_ref, r, 5) * m_ref[...]
        x_ref[...] = x
    else:
        x = x_ref[...]
    h = _rms(x) * g_ref[...] * (1.0 + _mod_chunk(mod_ref, r, 1)) + _mod_chunk(mod_ref, r, 0)
    z = jnp.dot(h.astype(BF16), wbf_ref[...], preferred_element_type=F32)
    za_ref[...] = z[:, 0:A_COLS]
    zb_ref[...] = z[:, A_COLS:A_COLS + B_COLS]
    zc_ref[...] = z[:, A_COLS + B_COLS:A_COLS + B_COLS + C_COLS]
    zd_ref[...] = z[:, A_COLS + B_COLS + C_COLS:N_IN]


def _inproj(layer, x, moe_out, mod, g, w_in):
    has_moe = moe_out is not None
    row = lambda i: (i, 0)
    widths = (A_COLS, B_COLS, C_COLS, D_COLS)
    tok_spec = pl.BlockSpec((TM, D_MODEL), row)
    in_specs = [tok_spec]
    args = [x]
    if has_moe:
        in_specs += [tok_spec, _mod_spec(layer - 1)]
        args += [moe_out, mod]
    in_specs += [_mod_spec(layer), pl.BlockSpec((1, D_MODEL), lambda i: (0, 0)),
                 pl.BlockSpec((None, D_MODEL, N_IN), lambda i: (layer, 0, 0))]
    args += [mod, g.reshape(1, D_MODEL), w_in]
    out_specs = [pl.BlockSpec((TM, wd), row) for wd in widths]
    out_shape = [jax.ShapeDtypeStruct((T_ALL, wd), F32) for wd in widths]
    if has_moe:
        out_specs = [tok_spec] + out_specs
        out_shape = [jax.ShapeDtypeStruct((T_ALL, D_MODEL), F32)] + out_shape
    outs = pl.pallas_call(
        functools.partial(_inproj_body, has_moe),
        grid=(T_ALL // TM,),
        in_specs=in_specs, out_specs=out_specs, out_shape=out_shape,
        scratch_shapes=[pltpu.VMEM((D_MODEL, N_IN), BF16)],
        compiler_params=_cparams(1), name="inproj",
    )(*args)
    return list(outs) if has_moe else [x] + list(outs)


HG_SCRATCH = 8


def _hgrn_body(layer, seq_len, zero_init, *refs):
    if zero_init:
        za_ref, lbl_ref, ng_ref, o_ref, sf_ref = refs[:5]
        s0_ref = None
        scr = refs[5:]
    else:
        za_ref, lbl_ref, ng_ref, s0_ref, o_ref, sf_ref = refs[:6]
        scr = refs[6:]
    qs_refs, ke_refs, tot_refs, acc_refs, s_ref = scr[0:2], scr[2:4], scr[4:6], scr[6:8], scr[8]
    L = seq_len
    n_chunks = L // HG_CHUNK
    ones = _block_ones(DG, HD).astype(BF16)
    bd = _block_ones(DG, HD).astype(F32)

    lg = lbl_ref[...]
    e = jnp.exp(lg - jnp.max(lg, axis=0, keepdims=True))
    p = e / jnp.sum(e, axis=0, keepdims=True)
    lb = jnp.zeros((1, 2 * DG), F32)
    for m in range(1, layer + 1):
        lb = lb + p[m:m + 1, :]

    q = _silu(za_ref[:, 0:DG])
    v = za_ref[:, 3 * DG:4 * DG]
    j = _iota((L, DG), 0) % HG_CHUNK

    for d in range(2):
        lbd = lb[:, d * DG:(d + 1) * DG]
        f = lbd + (1.0 - lbd) * _sigmoid(za_ref[:, (1 + d) * DG:(2 + d) * DG])
        k = 1.0 - f
        lf = jnp.log(f)
        fc, rc = _chunk_cumsums(lf, HG_CHUNK)
        lam, rest = (fc, rc - lf) if d == 0 else (rc, fc - lf)
        qs_refs[d][...] = q * jnp.exp(lam)
        ke_refs[d][...] = k * jnp.exp(rest)
        tot_refs[d][...] = fc + rc - lf

        o = _head_sum(q * k, ones) * v
        for lag in range(1, HG_CHUNK):
            sgn = lag if d == 0 else -lag
            valid = (j >= lag) if d == 0 else (j < HG_CHUNK - lag)
            ex = jnp.exp(jnp.where(valid, lam - _shift(lam, sgn), NEG_BIG))
            o = o + _mm(q * _shift(k, sgn) * ex, ones) * _shift(v, sgn)
        acc_refs[d][...] = o
        if zero_init:
            s_ref[d] = jnp.zeros((DG, DG), F32)
        else:
            s_ref[d] = s0_ref[d]

    def chunk_step(c, carry):
        for d in range(2):
            ci = c if d == 0 else n_chunks - 1 - c
            r0 = pl.multiple_of(ci * HG_CHUNK, HG_CHUNK)
            sl = pl.ds(r0, HG_CHUNK)
            s = s_ref[d]
            acc_refs[d][sl, :] += _mm_nt(qs_refs[d][sl, :], s)
            dec = jnp.exp(tot_refs[d][pl.ds(r0, 1), :])
            upd = _mm_tn(za_ref[sl, 3 * DG:4 * DG], ke_refs[d][sl, :])
            s_ref[d] = s * dec + upd * bd
        return carry
    lax.fori_loop(0, n_chunks, chunk_step, 0, unroll=2)
    sf_ref[...] = s_ref[...]

    o = acc_refs[0][...] + acc_refs[1][...]
    ms = _head_sum(o * o, ones) * (1.0 / HD)
    o_ref[...] = o * lax.rsqrt(ms + EPS) * ng_ref[...] * _silu(za_ref[:, 4 * DG:5 * DG])


def _seq_specs(L, width, first_block):
    return pl.BlockSpec((L, width), lambda b: (first_block + b, 0))


def _lru_body(seq_len, zero_init, *refs):
    if zero_init:
        zb_ref, pv_ref, wa_ref, wx_ref, o_ref, hf_ref, a_ref, u_ref, acc_ref = refs
        h0_ref = None
    else:
        zb_ref, pv_ref, wa_ref, wx_ref, h0_ref, o_ref, hf_ref, a_ref, u_ref, acc_ref = refs
    L = seq_len
    n_chunks = L // SC_CHUNK
    rows = _iota((L, DG), 0)
    j = rows % SC_CHUNK
    xb = zb_ref[:, 0:DG]
    xc = (pv_ref[0:1, :] * jnp.where(rows >= 2, _shift(xb, 2), 0.0)
          + pv_ref[1:2, :] * jnp.where(rows >= 1, _shift(xb, 1), 0.0)
          + pv_ref[2:3, :] * xb
          + pv_ref[3:4, :] * jnp.where(rows < L - 1, _shift(xb, -1), 0.0)
          + pv_ref[4:5, :])
    acc_ref[...] = jnp.zeros((L, DG), F32)
    for d in range(2):
        ba, bx, lam = pv_ref[5 + d:6 + d, :], pv_ref[7 + d:8 + d, :], pv_ref[9 + d:10 + d, :]
        r = _sigmoid(_mm(xc, wa_ref[d]) + ba)
        i = _sigmoid(_mm(xc, wx_ref[d]) + bx)
        log_a = -LRU_C * r * _softplus(-lam)
        a = jnp.exp(log_a)
        u = jnp.sqrt(1.0 - jnp.exp(2.0 * log_a)) * (i * xc)
        s = 1
        while s < SC_CHUNK:
            sgn = s if d == 0 else -s
            valid = (j >= s) if d == 0 else (j < SC_CHUNK - s)
            u = u + a * jnp.where(valid, _shift(u, sgn), 0.0)
            a = a * jnp.where(valid, _shift(a, sgn), 1.0)
            s *= 2
        a_ref[...] = a
        u_ref[...] = u
        h0 = jnp.zeros((1, DG), F32) if zero_init else h0_ref[d:d + 1, :]

        def chunk_step(c, carry, d=d):
            ci = c if d == 0 else n_chunks - 1 - c
            r0 = pl.multiple_of(ci * SC_CHUNK, SC_CHUNK)
            h = u_ref[pl.ds(r0, SC_CHUNK), :] + a_ref[pl.ds(r0, SC_CHUNK), :] * carry
            acc_ref[pl.ds(r0, SC_CHUNK), :] += h
            return h[SC_CHUNK - 1:SC_CHUNK, :] if d == 0 else h[0:1, :]
        hf_ref[d:d + 1, :] = lax.fori_loop(0, n_chunks, chunk_step, h0)
    o_ref[...] = _gelu(zb_ref[:, DG:2 * DG]) * acc_ref[...]


def _mixer_call(body, name, zero_init, z, z_width, params, state0, state_shape, n_scratch_rows,
                extra_scratch, o_prev):
    L, nseq = (CTX_LEN, N_CTX_SEQ) if zero_init else (LAT_LEN, N_LAT_SEQ)
    first = 0 if zero_init else T_CTX // LAT_LEN
    in_specs = [_seq_specs(L, z_width, first)]
    args = [z]
    for p in params:
        in_specs.append(pl.BlockSpec(p.shape, lambda b, nd=p.ndim: (0,) * nd))
        args.append(p)
    nstate = len(state_shape)
    state_spec = pl.BlockSpec((None,) + state_shape, lambda b: (b,) + (0,) * nstate)
    aliases = {}
    kern = functools.partial(body, L, zero_init)
    if not zero_init:
        in_specs.append(state_spec)
        args.append(state0)
        in_specs.append(pl.BlockSpec(memory_space=pl.ANY))
        args.append(o_prev)
        aliases = {len(args) - 1: 0}
        n_in = len(args)
        inner = kern
        kern = lambda *refs: inner(*refs[:n_in - 1], *refs[n_in:])
    return pl.pallas_call(
        kern,
        grid=(nseq,),
        in_specs=in_specs,
        out_specs=[_seq_specs(L, DG, first), state_spec],
        out_shape=[jax.ShapeDtypeStruct((T_ALL, DG), F32),
                   jax.ShapeDtypeStruct((nseq,) + state_shape, F32)],
        scratch_shapes=[pltpu.VMEM((L, w), F32) for w in n_scratch_rows] + list(extra_scratch),
        input_output_aliases=aliases,
        compiler_params=_cparams(1), name=name + ("_ctx" if zero_init else "_lat"),
    )(*args)


RW_SCRATCH = 14


def _rwkv_body(seq_len, zero_init, *refs):
    if zero_init:
        zc_ref, pv_ref, wup_ref, aup_ref, gup_ref, o_ref, sf_ref = refs[:7]
        s0_ref = None
        scr = refs[7:]
    else:
        zc_ref, pv_ref, wup_ref, aup_ref, gup_ref, s0_ref, o_ref, sf_ref = refs[:8]
        scr = refs[8:]
    (kt_ref, rt_ref, kh_ref, bh_ref, kb_ref, bb_ref, tot_ref, v_ref, w_ref, u_ref, rh_ref, yh_ref,
     acc_ref, yf_ref, s_ref) = scr
    L = seq_len
    n_chunks = L // RW_CHUNK
    n_groups = L // RW_GROUP
    ones = _block_ones(DG, HD).astype(BF16)
    bd = _block_ones(DG, HD).astype(F32)
    rows = _iota((L, DG), 0)

    def pvr(i):
        return pv_ref[i:i + 1, :]

    def tshift(t, mu):
        prev = jnp.where(rows >= 1, _shift(t, 1), 0.0)
        nxt = jnp.where(rows < L - 1, _shift(t, -1), 0.0)
        return t + mu * (0.5 * (prev + nxt) - t)

    r = tshift(zc_ref[:, 0:DG], pvr(0))
    k = tshift(zc_ref[:, DG:2 * DG], pvr(1))
    v = tshift(zc_ref[:, 2 * DG:3 * DG], pvr(2))
    tail = zc_ref[:, 3 * DG:3 * DG + 128]
    kk = k * pvr(7)
    kk = kk / jnp.maximum(jnp.sqrt(_head_sum(kk * kk, ones)), 1e-12)
    gate = _mm(_sigmoid(tail), gup_ref[...])
    tanh_tail = jnp.tanh(tail)
    v_ref[...] = v
    acc_ref[...] = jnp.zeros((L, DG), F32)

    n_big = 4 * RW_GROUP
    ri, ci = _iota((n_big, n_big), 0), _iota((n_big, n_big), 1)
    same = ((ri // RW_GROUP) == (ci // RW_GROUP)) & (((ri % RW_GROUP) // RW_CHUNK) == ((ci % RW_GROUP) // RW_CHUNK))
    eye = (ri == ci).astype(F32)
    lane_head = _iota((RW_GROUP, DG), 1) // HD

    def pair_blocks(b):
        return ((ri // (2 * b)) == (ci // (2 * b))) & ((ri // b) != (ci // b))

    def stack_heads(x):
        return jnp.concatenate([jnp.where(lane_head == h, x, 0.0) for h in range(4)], axis=0)

    def fold_heads(x):
        return (x[0:RW_GROUP] + x[RW_GROUP:2 * RW_GROUP]
                + x[2 * RW_GROUP:3 * RW_GROUP] + x[3 * RW_GROUP:4 * RW_GROUP])

    for d in range(2):
        w_log = -_softplus(-(pvr(3 + d) + _mm(tanh_tail, wup_ref[d]))) - 0.5
        lw = -jnp.exp(w_log)
        a = _sigmoid(pvr(5 + d) + _mm(tail, aup_ref[d]))
        kd = k * (1.0 + (a - 1.0) * pvr(8))
        beta = kk * a
        acc_ref[...] += _head_sum(r * kd * pvr(9), ones) * v
        fc, rc = _chunk_cumsums(lw, RW_CHUNK)
        lam, rest = (fc, rc - lw) if d == 0 else (rc, fc - lw)
        kt_ref[...] = kk * jnp.exp(lam - lw)
        rt_ref[...] = r * jnp.exp(lam)
        inv = jnp.exp(-lam)
        kh_ref[...] = kd * inv
        bh_ref[...] = beta * inv
        ex_rest = jnp.exp(rest)
        kb_ref[...] = kd * ex_rest
        bb_ref[...] = beta * ex_rest
        tot_ref[...] = fc + rc - lw

        earlier = (ci < ri) if d == 0 else (ci > ri)
        m_strict = (same & earlier).astype(F32)
        m_incl = (same & (earlier | (ri == ci))).astype(F32)

        def local_step(g, carry):
            g0 = pl.multiple_of(g * RW_GROUP, RW_GROUP)
            sl = pl.ds(g0, RW_GROUP)
            lhs_k = stack_heads(kt_ref[sl, :])
            lhs_r = stack_heads(rt_ref[sl, :])
            rhs_b = jnp.concatenate([bh_ref[sl, :]] * 4, axis=0)
            rhs_k = jnp.concatenate([kh_ref[sl, :]] * 4, axis=0)
            v_big = stack_heads(v_ref[sl, :])
            n_mat = m_strict * _mm_nt(lhs_k, rhs_b)
            m_mat = m_strict * _mm_nt(lhs_k, rhs_k)
            a_rb = m_incl * _mm_nt(lhs_r, rhs_b)
            a_rk = m_incl * _mm_nt(lhs_r, rhs_k)
            t_mat = eye - jnp.where(pair_blocks(1), n_mat, 0.0)
            b = 2
            while b < RW_CHUNK:
                t_mat = t_mat - _mm(t_mat, _mm(jnp.where(pair_blocks(b), n_mat, 0.0), t_mat))
                b *= 2
            w_big = _mm(t_mat, lhs_k)
            u_big = _mm(t_mat, _mm(m_mat, v_big))
            w_ref[sl, :] = fold_heads(w_big)
            u_ref[sl, :] = fold_heads(u_big)
            rh_ref[sl, :] = fold_heads(lhs_r - _mm(a_rb, w_big))
            yh_ref[sl, :] = fold_heads(_mm(a_rk, v_big) - _mm(a_rb, u_big))
            return carry
        lax.fori_loop(0, n_groups, local_step, 0, unroll=2)

        if zero_init:
            s_ref[...] = jnp.zeros((DG, DG), F32)
        else:
            s_ref[...] = s0_ref[d]

        def chunk_step(c, carry, d=d):
            cidx = c if d == 0 else n_chunks - 1 - c
            r0 = pl.multiple_of(cidx * RW_CHUNK, RW_CHUNK)
            sl = pl.ds(r0, RW_CHUNK)
            s = s_ref[...]
            yh_ref[sl, :] = _mm_nt(rh_ref[sl, :], s) + yh_ref[sl, :]
            bb = bb_ref[sl, :].astype(BF16)
            phi = _mm_tn(w_ref[sl, :].astype(BF16), bb)
            delta = _mm_tn(jnp.concatenate([v_ref[sl, :], u_ref[sl, :]], axis=0).astype(BF16),
                           jnp.concatenate([kb_ref[sl, :].astype(BF16), -bb], axis=0))
            s_ref[...] = s * jnp.exp(tot_ref[pl.ds(r0, 1), :]) + bd * (delta - _mm(s, phi))
            return carry
        lax.fori_loop(0, n_chunks, chunk_step, 0, unroll=2)
        sf_ref[d] = s_ref[...]
        if d == 0:
            yf_ref[...] = yh_ref[...]
    y = yf_ref[...] + yh_ref[...]
    mu = _head_sum(y, ones) * (1.0 / HD)
    yc = y - mu
    var = _head_sum(yc * yc, ones) * (1.0 / HD)
    y = yc * lax.rsqrt(var + C_LN_EPS) * pvr(10) + pvr(11)
    o_ref[...] = (y + acc_ref[...]) * gate


S5_BLK = 256


def _s5_body(seq_len, zero_init, *refs):
    if zero_init:
        zd_ref, pa_ref, pv_ref, bre_ref, bim_ref, cre_ref, cim_ref, glu_ref, o_ref, xf_ref = refs[:10]
        x0_ref = None
        scr = refs[10:]
    else:
        (zd_ref, pa_ref, pv_ref, bre_ref, bim_ref, cre_ref, cim_ref, glu_ref, x0_ref,
         o_ref, xf_ref) = refs[:11]
        scr = refs[11:]
    xr_ref, xi_ref, acc_ref = scr
    L = seq_len
    n_chunks = L // SC_CHUNK
    n_blk = L // S5_BLK
    jb = _iota((S5_BLK, S5_N), 0) % SC_CHUNK
    j8 = _iota((SC_CHUNK, S5_N), 0)

    def cmul(ar, ai, br, bi):
        return ar * br - ai * bi, ar * bi + ai * br

    for d in range(2):
        a_re, a_im, ls = pa_ref[d, 0:1, :], pa_ref[d, 1:2, :], pa_ref[d, 2:3, :]
        step = jnp.exp(ls)
        mag = jnp.exp(step * a_re)
        p1r, p1i = mag * jnp.cos(step * a_im), mag * jnp.sin(step * a_im)
        den = a_re * a_re + a_im * a_im
        zr = ((p1r - 1.0) * a_re + p1i * a_im) / den
        zi = (p1i * a_re - (p1r - 1.0) * a_im) / den
        p2r, p2i = cmul(p1r, p1i, p1r, p1i)
        p4r, p4i = cmul(p2r, p2i, p2r, p2i)
        powers = ((1, p1r, p1i), (2, p2r, p2i), (4, p4r, p4i))
        pwr = jnp.broadcast_to(p1r, (SC_CHUNK, S5_N))
        pwi = jnp.broadcast_to(p1i, (SC_CHUNK, S5_N))
        for s, _, _ in powers:
            sgn = s if d == 0 else -s
            valid = (j8 >= s) if d == 0 else (j8 < SC_CHUNK - s)
            nr, ni = cmul(pwr, pwi, _shift(pwr, sgn), _shift(pwi, sgn))
            pwr, pwi = jnp.where(valid, nr, pwr), jnp.where(valid, ni, pwi)

        def blk_step(bi, carry, d=d, zr=zr, zi=zi, powers=powers):
            sl = pl.ds(pl.multiple_of(bi * S5_BLK, S5_BLK), S5_BLK)
            u = zd_ref[sl, :]
            m1 = _mm(u, bre_ref[d])
            m2 = _mm(u, bim_ref[d])
            xr = zr * m1 - zi * m2
            xi = zr * m2 + zi * m1
            for s, pr, pi in powers:
                sgn = s if d == 0 else -s
                valid = (jb >= s) if d == 0 else (jb < SC_CHUNK - s)
                sr = jnp.where(valid, _shift(xr, sgn), 0.0)
                si = jnp.where(valid, _shift(xi, sgn), 0.0)
                xr, xi = xr + pr * sr - pi * si, xi + pr * si + pi * sr
            xr_ref[sl, :] = xr
            xi_ref[sl, :] = xi
            return carry
        lax.fori_loop(0, n_blk, blk_step, 0)

        if zero_init:
            c0 = (jnp.zeros((1, S5_N), F32), jnp.zeros((1, S5_N), F32))
        else:
            c0 = (x0_ref[d, 0:1, :], x0_ref[d, 1:2, :])

        def chunk_step(c, carry, d=d, pwr=pwr, pwi=pwi):
            ci = c if d == 0 else n_chunks - 1 - c
            sl = pl.ds(pl.multiple_of(ci * SC_CHUNK, SC_CHUNK), SC_CHUNK)
            cr, cim = carry
            hr = xr_ref[sl, :] + pwr * cr - pwi * cim
            hi = xi_ref[sl, :] + pwr * cim + pwi * cr
            xr_ref[sl, :] = hr
            xi_ref[sl, :] = hi
            last = SC_CHUNK - 1 if d == 0 else 0
            return hr[last:last + 1, :], hi[last:last + 1, :]
        fr, fi = lax.fori_loop(0, n_chunks, chunk_step, c0)
        xf_ref[d, 0:1, :] = fr
        xf_ref[d, 1:2, :] = fi

        def out_step(bi, carry, d=d):
            sl = pl.ds(pl.multiple_of(bi * S5_BLK, S5_BLK), S5_BLK)
            y = _mm(xr_ref[sl, :], cre_ref[d]) - _mm(xi_ref[sl, :], cim_ref[d])
            if d == 0:
                acc_ref[sl, :] = y
            else:
                acc_ref[sl, :] += y
            return carry
        lax.fori_loop(0, n_blk, out_step, 0)

    y = _gelu(acc_ref[...] + pv_ref[0:1, :] * zd_ref[...])
    o_ref[...] = y * _sigmoid(_mm(y, glu_ref[...]) + pv_ref[1:2, :])


def _outproj_body(oa_ref, ob_ref, oc_ref, od_ref, x_ref, mod_ref, mg_ref, w_ref, g2_ref, rw_ref, rb_ref,
                  x1_ref, h2_ref, route_ref, wbf_ref):
    i = pl.program_id(0)

    @pl.when(i == 0)
    def _():
        wbf_ref[...] = w_ref[...].astype(BF16)

    r = _mod_row(i)
    y = jnp.zeros((TM, D_MODEL), F32)
    for n, ref in enumerate((oa_ref, ob_ref, oc_ref, od_ref)):
        on = _rms(ref[...]) * mg_ref[:, n * DG:(n + 1) * DG]
        y = y + jnp.dot(on.astype(BF16), wbf_ref[n * DG:(n + 1) * DG, :], preferred_element_type=F32)
    x1 = x_ref[...] + _mod_chunk(mod_ref, r, 2) * y
    x1_ref[...] = x1
    h2 = _rms(x1) * g2_ref[...] * (1.0 + _mod_chunk(mod_ref, r, 4)) + _mod_chunk(mod_ref, r, 3)
    h2_ref[...] = h2.astype(BF16)
    logits = jnp.dot(h2, rw_ref[...], precision=lax.Precision.HIGHEST,
                     preferred_element_type=F32) + rb_ref[...]
    lane = _iota((TM, N_EXPERTS), 1).astype(F32)
    col = _iota((TM, 2 * TOP_K), 1)
    work = logits
    top = jnp.max(work, axis=-1, keepdims=True)
    route = jnp.zeros((TM, 2 * TOP_K), F32)
    denom = jnp.zeros((TM, 1), F32)
    for k in range(TOP_K):
        m = jnp.max(work, axis=-1, keepdims=True)
        idx = jnp.min(jnp.where(work == m, lane, float(N_EXPERTS)), axis=-1, keepdims=True)
        pe = jnp.exp(m - top)
        route = route + jnp.where(col == k, pe, 0.0) + jnp.where(col == TOP_K + k, idx, 0.0)
        denom = denom + pe
        work = jnp.where(lane == idx, -jnp.inf, work)
    route_ref[...] = jnp.where(col < TOP_K, route / denom, route)


def _outproj(layer, o_mix, x, mod, mg, w_out, g2, router_w, rb):
    row = lambda i: (i, 0)
    full = lambda i: (0, 0)
    return pl.pallas_call(
        _outproj_body,
        grid=(T_ALL // TM,),
        in_specs=[pl.BlockSpec((TM, DG), row)] * 4 + [
            pl.BlockSpec((TM, D_MODEL), row),
            _mod_spec(layer),
            pl.BlockSpec((1, D_MODEL), full),
            pl.BlockSpec((None, D_MODEL, D_MODEL), lambda i: (layer, 0, 0)),
            pl.BlockSpec((1, D_MODEL), full),
            pl.BlockSpec((None, D_MODEL, N_EXPERTS), lambda i: (layer, 0, 0)),
            pl.BlockSpec((1, N_EXPERTS), full)],
        out_specs=[pl.BlockSpec((TM, D_MODEL), row), pl.BlockSpec((TM, D_MODEL), row),
                   pl.BlockSpec((TM, 2 * TOP_K), row)],
        out_shape=[jax.ShapeDtypeStruct((T_ALL, D_MODEL), F32),
                   jax.ShapeDtypeStruct((T_ALL, D_MODEL), BF16),
                   jax.ShapeDtypeStruct((T_ALL, 2 * TOP_K), F32)],
        scratch_shapes=[pltpu.VMEM((D_MODEL, D_MODEL), BF16)],
        compiler_params=_cparams(1), name="outproj_router",
    )(*o_mix, x, mod, mg.reshape(1, D_MODEL), w_out, g2.reshape(1, D_MODEL), router_w,
      rb.reshape(1, N_EXPERTS))


def _route_tables(route):
    prob = route[:, :TOP_K].reshape(N_PARTS, MOE_SLOTS)
    eid = route[:, TOP_K:].astype(jnp.int32).reshape(N_PARTS, MOE_SLOTS)
    slot = jnp.arange(MOE_SLOTS, dtype=jnp.int32)
    experts = jnp.arange(N_EXPERTS, dtype=jnp.int32)
    cnt = jnp.sum((eid[:, :, None] == experts[None, None, :]).astype(jnp.int32), axis=1)
    zero = jnp.zeros((N_PARTS, 1), jnp.int32)
    toffs = jnp.concatenate([zero, jnp.cumsum((cnt + MOE_TILE - 1) // MOE_TILE, axis=1)], axis=1)
    stride = MOE_SLOTS + MOE_TILE
    last = N_EXPERTS * stride
    n_dummy = MOE_TILE - 1
    dummy_j = jnp.arange(n_dummy, dtype=jnp.int32)
    need = (-cnt) % MOE_TILE
    dummy_key = jnp.where(dummy_j[None, None, :] < need[:, :, None],
                          experts[None, :, None] * stride + MOE_SLOTS + dummy_j[None, None, :], last)
    n_fill = MOE_TILES * MOE_TILE - MOE_SLOTS - N_EXPERTS * n_dummy
    keys = jnp.concatenate([eid * stride + slot[None, :], dummy_key.reshape(N_PARTS, N_EXPERTS * n_dummy),
                            jnp.full((N_PARTS, n_fill), last, jnp.int32)], axis=1)
    n_pad = MOE_TILES * MOE_TILE - MOE_SLOTS
    tok = jnp.concatenate([jnp.broadcast_to(slot // TOP_K, (N_PARTS, MOE_SLOTS)),
                           jnp.full((N_PARTS, n_pad), -1, jnp.int32)], axis=1)
    wgt = jnp.concatenate([prob, jnp.zeros((N_PARTS, n_pad), F32)], axis=1)
    _, tok, wgt = lax.sort((keys, tok, wgt), dimension=1, num_keys=1)
    return (toffs.reshape(-1), tok.reshape(N_PARTS * MOE_TILES, MOE_TILE),
            wgt.reshape(N_PARTS * MOE_TILES, MOE_TILE))


def _moe_body(toffs_ref, h_ref, tok_ref, wgt_ref, wgu_ref, bgu_ref, wdn_ref, bdn_ref, o_ref):
    p, e = pl.program_id(0), pl.program_id(1)

    @pl.when(e == 0)
    def _():
        o_ref[...] = jnp.zeros((MOE_PART, D_MODEL), F32)

    ones = jnp.ones((MOE_TILE, MOE_TILE), BF16)
    eye = _iota((MOE_TILE, MOE_TILE), 0) == _iota((MOE_TILE, MOE_TILE), 1)
    lane_tok = _iota((MOE_TILE, MOE_PART), 1)
    lane_hi = (lane_tok // 64).astype(F32)
    lane_lo = (lane_tok % 64).astype(F32)
    row_tok = _iota((MOE_PART, MOE_TILE), 0)
    row_hi = (row_tok // 64).astype(F32)
    row_lo = (row_tok % 64).astype(F32)

    def to_column(row):
        return jnp.dot(jnp.where(eye, row, 0.0).astype(BF16), ones, preferred_element_type=F32)

    def tile_step(ti, carry):
        tokr = tok_ref[pl.ds(ti, 1), :].astype(F32)
        hi = jnp.floor(tokr * (1.0 / 64.0))
        lo = tokr - 64.0 * hi
        hi_c = jnp.concatenate([to_column(hi)] * (MOE_PART // MOE_TILE), axis=1)
        lo_c = jnp.concatenate([to_column(lo)] * (MOE_PART // MOE_TILE), axis=1)
        gather = ((lane_hi == hi_c) & (lane_lo == lo_c)).astype(BF16)
        xs = jnp.dot(gather, h_ref[...], preferred_element_type=F32).astype(BF16)
        gu = jnp.dot(xs, wgu_ref[...], preferred_element_type=F32) + bgu_ref[...]
        gate = jnp.minimum(gu[:, :D_FF], SWIGLU_LIMIT)
        up = jnp.clip(gu[:, D_FF:], -SWIGLU_LIMIT, SWIGLU_LIMIT)
        act = (up + 1.0) * gate * _sigmoid(SWIGLU_ALPHA * gate)
        ys = jnp.dot(act.astype(BF16), wdn_ref[...], preferred_element_type=F32) + bdn_ref[...]
        wr = wgt_ref[pl.ds(ti, 1), :]
        w_hi = wr.astype(BF16).astype(F32)
        w_col = to_column(w_hi) + to_column(wr - w_hi)
        ysw = (ys * jnp.concatenate([w_col] * (D_MODEL // MOE_TILE), axis=1)).astype(BF16)
        scatter = ((row_hi == hi) & (row_lo == lo)).astype(BF16)
        for c in range(MOE_PART // 512):
            o_ref[c * 512:(c + 1) * 512, :] += jnp.dot(scatter[c * 512:(c + 1) * 512, :], ysw,
                                                      preferred_element_type=F32)
        return carry

    lax.fori_loop(toffs_ref[p * (N_EXPERTS + 1) + e], toffs_ref[p * (N_EXPERTS + 1) + e + 1], tile_step, 0)


def _moe(layer, h2, toffs, tok, wgt, w_gu, b_gu, w_dn, b_dn):
    grid_spec = pltpu.PrefetchScalarGridSpec(
        num_scalar_prefetch=1,
        grid=(N_PARTS, N_EXPERTS),
        in_specs=[pl.BlockSpec((MOE_PART, D_MODEL), lambda p, e, t: (p, 0)),
                  pl.BlockSpec((MOE_TILES, MOE_TILE), lambda p, e, t: (p, 0)),
                  pl.BlockSpec((MOE_TILES, MOE_TILE), lambda p, e, t: (p, 0)),
                  pl.BlockSpec((None, None, D_MODEL, 2 * D_FF), lambda p, e, t: (layer, e, 0, 0)),
                  pl.BlockSpec((None, None, 1, 2 * D_FF), lambda p, e, t: (layer, e, 0, 0)),
                  pl.BlockSpec((None, None, D_FF, D_MODEL), lambda p, e, t: (layer, e, 0, 0)),
                  pl.BlockSpec((None, None, 1, D_MODEL), lambda p, e, t: (layer, e, 0, 0))],
        out_specs=pl.BlockSpec((MOE_PART, D_MODEL), lambda p, e, t: (p, 0)),
    )
    return pl.pallas_call(
        _moe_body,
        grid_spec=grid_spec,
        out_shape=jax.ShapeDtypeStruct((T_ALL, D_MODEL), F32),
        compiler_params=_cparams(2), name="moe",
    )(toffs, h2, tok, wgt, w_gu, b_gu.reshape(DEPTH, N_EXPERTS, 1, 2 * D_FF), w_dn,
      b_dn.reshape(DEPTH, N_EXPERTS, 1, D_MODEL))


def _final_body(x1_ref, m_ref, mod_ref, g_ref, o_ref):
    r = _mod_row(pl.program_id(0))
    o_ref[...] = _rms(x1_ref[...] + _mod_chunk(mod_ref, r, 5) * m_ref[...]) * g_ref[...]


def _final_norm(x1, moe_out, mod, g):
    tok_spec = pl.BlockSpec((TM, D_MODEL), lambda i: (i, 0))
    return pl.pallas_call(
        _final_body,
        grid=(T_ALL // TM,),
        in_specs=[tok_spec, tok_spec, _mod_spec(DEPTH - 1), pl.BlockSpec((1, D_MODEL), lambda i: (0, 0))],
        out_specs=tok_spec,
        out_shape=jax.ShapeDtypeStruct((T_ALL, D_MODEL), F32),
        compiler_params=_cparams(1), name="final_norm",
    )(x1, moe_out, mod, g.reshape(1, D_MODEL))


def _to_block_diag(s):
    eye = jnp.eye(4, dtype=s.dtype)
    out = jnp.einsum('hg,...hab->...hagb', eye, s)
    return out.reshape(s.shape[:-3] + (DG, DG))


def _from_block_diag(m):
    blocks = m.reshape(m.shape[:-2] + (4, HD, 4, HD))
    return jnp.stack([blocks[..., h, :, h, :] for h in range(4)], axis=-3)


def _pad_rows(w, first, total=128):
    pad = [(0, 0)] * (w.ndim - 2) + [(first, total - first - w.shape[-2]), (0, 0)]
    return jnp.pad(w, pad)


def kernel(x_prompt, x_sample, c, state_hgrn, state_lru, state_rwkv, state_s5, c_ctx, ada_w, ada_b, norm1_g, norm2_g, w_in, w_out, merge_g, hgrn_lb_logits, hgrn_norm_g, lru_conv_w, lru_conv_b, lru_wa, lru_ba, lru_wx, lru_bx, lru_lambda, rwkv_mu, rwkv_w0, rwkv_w_up, rwkv_a0, rwkv_a_up, rwkv_g_up, rwkv_k_k, rwkv_k_a, rwkv_r_k, rwkv_ln_g, rwkv_ln_b, s5_a_re, s5_a_im, s5_log_step, s5_b_re, s5_b_im, s5_c_re, s5_c_im, s5_d, s5_glu_w, s5_glu_b, router_w, router_b, moe_w_gu, moe_b_gu, moe_w_down, moe_b_down, final_norm_g):
    cvec = jnp.concatenate([c_ctx[None], c, jnp.zeros((5, D_MODEL), F32)], axis=0)
    x_cat = jnp.concatenate([x_prompt.reshape(T_CTX, D_MODEL),
                             x_sample.reshape(N_LAT_SEQ * LAT_LEN, D_MODEL)], axis=0)
    lbl = hgrn_lb_logits.reshape(DEPTH, 2 * DG)
    hg_s0 = _to_block_diag(jnp.swapaxes(state_hgrn, -1, -2))
    rw_s0 = _to_block_diag(state_rwkv)
    s5_x0 = jnp.moveaxis(state_s5.reshape(N_LAT_SEQ, DEPTH, 2, S5_N, 2), -1, -2)
    lru_pv = jnp.concatenate([lru_conv_w, lru_conv_b[:, None], lru_ba, lru_bx, lru_lambda,
                              jnp.zeros((DEPTH, 5, DG), F32)], axis=1)
    lru_wa_bd = _to_block_diag(lru_wa)
    lru_wx_bd = _to_block_diag(lru_wx)
    rw_pv = jnp.concatenate([rwkv_mu, rwkv_w0, rwkv_a0, rwkv_k_k[:, None], rwkv_k_a[:, None],
                             rwkv_r_k[:, None], rwkv_ln_g[:, None], rwkv_ln_b[:, None],
                             jnp.zeros((DEPTH, 4, DG), F32)], axis=1)
    rw_wup = _pad_rows(rwkv_w_up, 0)
    rw_aup = _pad_rows(rwkv_a_up, 32)
    rw_gup = _pad_rows(rwkv_g_up, 64)
    s5_pa = jnp.stack([s5_a_re.reshape(DEPTH, 2, S5_N), s5_a_im.reshape(DEPTH, 2, S5_N),
                       jnp.repeat(s5_log_step, 64, axis=-1)], axis=2)
    s5_pa = jnp.pad(s5_pa, ((0, 0), (0, 0), (0, 5), (0, 0)))
    s5_pv = jnp.concatenate([s5_d[:, None], s5_glu_b[:, None], jnp.zeros((DEPTH, 6, DG), F32)], axis=1)
    eye16 = jnp.eye(16, dtype=F32)
    s5_bre = jnp.einsum('gh,ldgpc->ldgchp', eye16, s5_b_re).reshape(DEPTH, 2, DG, S5_N)
    s5_bim = jnp.einsum('gh,ldgpc->ldgchp', eye16, s5_b_im).reshape(DEPTH, 2, DG, S5_N)
    s5_cre = jnp.einsum('gh,ldgcp->ldgphc', eye16, s5_c_re).reshape(DEPTH, 2, S5_N, DG)
    s5_cim = jnp.einsum('gh,ldgcp->ldgphc', eye16, s5_c_im).reshape(DEPTH, 2, S5_N, DG)

    mod = _ada_table(cvec, ada_w, ada_b)
    x1 = _embed(x_cat, _grid_pos_table())
    moe_out = None
    w_gu_bf = moe_w_gu.astype(BF16)
    w_dn_bf = moe_w_down.astype(BF16)

    new_hgrn, new_lru, new_rwkv, new_s5 = [], [], [], []
    for l in range(DEPTH):
        x, za, zb, zc, zd = _inproj(l, x1, moe_out, mod, norm1_g[l], w_in)

        hg_body = functools.partial(_hgrn_body, l)
        hg_params = (lbl, hgrn_norm_g[l].reshape(1, DG))
        hg_extra = (pltpu.VMEM((2, DG, DG), F32),)
        oa, sa = _mixer_call(hg_body, "hgrn", True, za, A_COLS, hg_params, None, (2, DG, DG),
                             (DG,) * HG_SCRATCH, hg_extra, None)
        oa, _ = _mixer_call(hg_body, "hgrn", False, za, A_COLS, hg_params, hg_s0[:, l], (2, DG, DG),
                            (DG,) * HG_SCRATCH, hg_extra, oa)

        lru_params = (lru_pv[l], lru_wa_bd[l], lru_wx_bd[l])
        ob, sb = _mixer_call(_lru_body, "lru", True, zb, B_COLS, lru_params, None, (2, DG),
                             (DG, DG, DG), (), None)
        ob, _ = _mixer_call(_lru_body, "lru", False, zb, B_COLS, lru_params, state_lru[:, l], (2, DG),
                            (DG, DG, DG), (), ob)

        rw_params = (rw_pv[l], rw_wup[l], rw_aup[l], rw_gup[l])
        rw_extra = (pltpu.VMEM((DG, DG), F32),)
        oc, sc = _mixer_call(_rwkv_body, "rwkv", True, zc, C_COLS, rw_params, None, (2, DG, DG),
                             (DG,) * RW_SCRATCH, rw_extra, None)
        oc, _ = _mixer_call(_rwkv_body, "rwkv", False, zc, C_COLS, rw_params, rw_s0[:, l], (2, DG, DG),
                            (DG,) * RW_SCRATCH, rw_extra, oc)

        s5_params = (s5_pa[l], s5_pv[l], s5_bre[l], s5_bim[l], s5_cre[l], s5_cim[l], s5_glu_w[l])
        od, sd = _mixer_call(_s5_body, "s5", True, zd, D_COLS, s5_params, None, (2, 2, S5_N),
                             (S5_N, S5_N, DG), (), None)
        od, _ = _mixer_call(_s5_body, "s5", False, zd, D_COLS, s5_params, s5_x0[:, l], (2, 2, S5_N),
                            (S5_N, S5_N, DG), (), od)

        x1, h2, route = _outproj(l, (oa, ob, oc, od), x, mod, merge_g[l], w_out, norm2_g[l],
                                 router_w, router_b[l])
        toffs, tok, wgt = _route_tables(route)
        moe_out = _moe(l, h2, toffs, tok, wgt, w_gu_bf, moe_b_gu, w_dn_bf, moe_b_down)
        new_hgrn.append(sa)
        new_lru.append(sb)
        new_rwkv.append(sc)
        new_s5.append(sd)

    y = _final_norm(x1, moe_out, mod, final_norm_g)
    y_prompt = y[:T_CTX].reshape(N_CTX_SEQ, CTX_LEN, D_MODEL)
    y_sample = y[T_CTX:].reshape(N_LAT_SEQ, LAT_LEN, D_MODEL)
    st_hgrn = jnp.swapaxes(_from_block_diag(jnp.stack(new_hgrn, axis=1)), -1, -2)
    st_lru = jnp.stack(new_lru, axis=1)
    st_rwkv = _from_block_diag(jnp.stack(new_rwkv, axis=1))
    st_s5 = jnp.moveaxis(jnp.stack(new_s5, axis=1), -2, -1).reshape(N_CTX_SEQ, DEPTH, 2, 16, 64, 2)
    return (y_prompt, y_sample, st_hgrn, st_lru, st_rwkv, st_s5)
```

```python
import functools
import math

import jax
import jax.numpy as jnp
from jax import lax
from jax.experimental import pallas as pl
from jax.experimental.pallas import tpu as pltpu

F32 = jnp.float32
BF16 = jnp.bfloat16

D_MODEL = 1024
DEPTH = 4
N_CTX_SEQ, CTX_LEN = 16, 256
N_LAT_SEQ, LAT_LEN = 2, 1024
T_CTX = N_CTX_SEQ * CTX_LEN
T_ALL = T_CTX + N_LAT_SEQ * LAT_LEN
GRID_W = 64
DG = 256
HD = 64
EPS = 1e-6
A_COLS, B_COLS, C_COLS, D_COLS = 5 * DG, 2 * DG, 3 * DG + 128, DG
N_IN = A_COLS + B_COLS + C_COLS + D_COLS
HG_CHUNK = 8
RW_CHUNK = 64
RW_GROUP = 64
SC_CHUNK = 8
LRU_C = 8.0
C_LN_EPS = 64e-5
S5_N = 1024
N_EXPERTS = 32
TOP_K = 4
D_FF = 1024
SWIGLU_ALPHA = 1.702
SWIGLU_LIMIT = 7.0
TM = 256
MOE_PART = 1536
N_PARTS = T_ALL // MOE_PART
MOE_SLOTS = MOE_PART * TOP_K
MOE_TILE = 256
MOE_TILES = MOE_SLOTS // MOE_TILE + N_EXPERTS
VMEM_LIMIT = 56 * 1024 * 1024
NEG_BIG = -1e30


def _cparams(n_axes):
    return pltpu.CompilerParams(dimension_semantics=("arbitrary",) * n_axes,
                                vmem_limit_bytes=VMEM_LIMIT)


def _iota(shape, axis):
    return lax.broadcasted_iota(jnp.int32, shape, axis)


def _shift(x, d):
    n = x.shape[0]
    d = d % n
    return x if d == 0 else pltpu.roll(x, d, 0)


def _mm(a, b):
    return jnp.dot(a.astype(BF16), b.astype(BF16), preferred_element_type=F32)


def _mm_nt(a, b):
    return lax.dot_general(a.astype(BF16), b.astype(BF16), (((1,), (1,)), ((), ())),
                           preferred_element_type=F32)


def _mm_tn(a, b):
    return lax.dot_general(a, b, (((0,), (0,)), ((), ())), preferred_element_type=F32)


def _block_ones(n, blk):
    r = _iota((n, n), 0) // blk
    c = _iota((n, n), 1) // blk
    return r == c


def _fold_diag_blocks(s):
    return s[0:HD] + s[HD:2 * HD] + s[2 * HD:3 * HD] + s[3 * HD:4 * HD]


def _head_sum(x, ones_bf16):
    hi = x.astype(BF16)
    lo = (x - hi.astype(F32)).astype(BF16)
    return (jnp.dot(hi, ones_bf16, preferred_element_type=F32)
            + jnp.dot(lo, ones_bf16, preferred_element_type=F32))


def _sigmoid(x):
    return 1.0 / (1.0 + jnp.exp(-x))


def _silu(x):
    return x * _sigmoid(x)


def _softplus(x):
    return jnp.maximum(x, 0.0) + jnp.log(1.0 + jnp.exp(-jnp.abs(x)))


def _gelu(x):
    c = math.sqrt(2.0 / math.pi)
    return 0.5 * x * (1.0 + jnp.tanh(c * (x + 0.044715 * (x * x * x))))


def _rms(x, eps=EPS):
    return x * lax.rsqrt(jnp.mean(x * x, axis=-1, keepdims=True) + eps)


def _chunk_cumsums(x, chunk):
    j = _iota(x.shape, 0) % chunk
    f, r = x, x
    s = 1
    while s < chunk:
        f = f + jnp.where(j >= s, _shift(f, s), 0.0)
        r = r + jnp.where(j < chunk - s, _shift(r, -s), 0.0)
        s *= 2
    return f, r


def _mod_row(i):
    start = i * TM
    return jnp.where(start < T_CTX, 0, 1 + (start - T_CTX) // LAT_LEN)


def _ada_body(c_ref, w_ref, b_ref, o_ref):
    o_ref[...] = _mm(_silu(c_ref[...]), w_ref[...]) + b_ref[...]


def _ada_table(cvec, ada_w, ada_b):
    nb = 6 * D_MODEL // 1024
    return pl.pallas_call(
        _ada_body,
        grid=(DEPTH, nb),
        in_specs=[pl.BlockSpec((8, D_MODEL), lambda l, j: (0, 0)),
                  pl.BlockSpec((None, D_MODEL, 1024), lambda l, j: (l, 0, j)),
                  pl.BlockSpec((None, 1, 1024), lambda l, j: (l, 0, j))],
        out_specs=pl.BlockSpec((None, 8, 1024), lambda l, j: (l, 0, j)),
        out_shape=jax.ShapeDtypeStruct((DEPTH, 8, 6 * D_MODEL), F32),
        compiler_params=_cparams(2), name="ada_table",
    )(cvec, ada_w, ada_b.reshape(DEPTH, 1, 6 * D_MODEL))


def _embed_body(x_ref, p_ref, o_ref):
    i = pl.program_id(0)

    @pl.when(i * TM < T_CTX)
    def _():
        o_ref[...] = x_ref[...]

    @pl.when(i * TM >= T_CTX)
    def _():
        o_ref[...] = x_ref[...] + p_ref[...]


def _embed(x_cat, pos):
    per_seq = LAT_LEN // TM
    first = T_CTX // TM
    return pl.pallas_call(
        _embed_body,
        grid=(T_ALL // TM,),
        in_specs=[pl.BlockSpec((TM, D_MODEL), lambda i: (i, 0)),
                  pl.BlockSpec((TM, D_MODEL), lambda i: (jnp.maximum(i - first, 0) % per_seq, 0))],
        out_specs=pl.BlockSpec((TM, D_MODEL), lambda i: (i, 0)),
        out_shape=jax.ShapeDtypeStruct((T_ALL, D_MODEL), F32),
        compiler_params=_cparams(1), name="embed",
    )(x_cat, pos)


def _grid_pos_table():
    t = jnp.arange(LAT_LEN)
    row = (t // GRID_W).astype(F32)
    col = (t % GRID_W).astype(F32)
    quarter = D_MODEL // 4
    freq = jnp.exp(-math.log(10000.0) * jnp.arange(quarter, dtype=F32) / quarter)

    def enc(p):
        ang = p[:, None] * freq[None, :]
        return jnp.concatenate([jnp.sin(ang), jnp.cos(ang)], axis=-1)
    return jnp.concatenate([enc(row), enc(col)], axis=-1)


def _mod_chunk(mod_ref, row, n):
    return mod_ref[pl.ds(row, 1), n * D_MODEL:(n + 1) * D_MODEL]


def _mod_spec(layer):
    return pl.BlockSpec((None, 8, 6 * D_MODEL), lambda i: (layer, 0, 0))


def _inproj_body(has_moe, *refs):
    if has_moe:
        (x1_ref, m_ref, pmod_ref, mod_ref, g_ref, w_ref,
         x_ref, za_ref, zb_ref, zc_ref, zd_ref, wbf_ref) = refs
    else:
        x_ref, mod_ref, g_ref, w_ref, za_ref, zb_ref, zc_ref, zd_ref, wbf_ref = refs
    i = pl.program_id(0)

    @pl.when(i == 0)
    def _():
        wbf_ref[...] = w_ref[...].astype(BF16)

    r = _mod_row(i)
    if has_moe:
        x = x1_ref[...] + _mod_chunk(pmod_ref, r, 5) * m_ref[...]
        x_ref[...] = x
    else:
        x = x_ref[...]
    h = _rms(x) * g_ref[...] * (1.0 + _mod_chunk(mod_ref, r, 1)) + _mod_chunk(mod_ref, r, 0)
    z = jnp.dot(h.astype(BF16), wbf_ref[...], preferred_element_type=F32)
    za_ref[...] = z[:, 0:A_COLS]
    zb_ref[...] = z[:, A_COLS:A_COLS + B_COLS]
    zc_ref[...] = z[:, A_COLS + B_COLS:A_COLS + B_COLS + C_COLS]
    zd_ref[...] = z[:, A_COLS + B_COLS + C_COLS:N_IN]


def _inproj(layer, x, moe_out, mod, g, w_in):
    has_moe = moe_out is not None
    row = lambda i: (i, 0)
    widths = (A_COLS, B_COLS, C_COLS, D_COLS)
    tok_spec = pl.BlockSpec((TM, D_MODEL), row)
    in_specs = [tok_spec]
    args = [x]
    if has_moe:
        in_specs += [tok_spec, _mod_spec(layer - 1)]
        args += [moe_out, mod]
    in_specs += [_mod_spec(layer), pl.BlockSpec((1, D_MODEL), lambda i: (0, 0)),
                 pl.BlockSpec((None, D_MODEL, N_IN), lambda i: (layer, 0, 0))]
    args += [mod, g.reshape(1, D_MODEL), w_in]
    out_specs = [pl.BlockSpec((TM, wd), row) for wd in widths]
    out_shape = [jax.ShapeDtypeStruct((T_ALL, wd), F32) for wd in widths]
    if has_moe:
        out_specs = [tok_spec] + out_specs
        out_shape = [jax.ShapeDtypeStruct((T_ALL, D_MODEL), F32)] + out_shape
    outs = pl.pallas_call(
        functools.partial(_inproj_body, has_moe),
        grid=(T_ALL // TM,),
        in_specs=in_specs, out_specs=out_specs, out_shape=out_shape,
        scratch_shapes=[pltpu.VMEM((D_MODEL, N_IN), BF16)],
        compiler_params=_cparams(1), name="inproj",
    )(*args)
    return list(outs) if has_moe else [x] + list(outs)


HG_SCRATCH = 8


def _hgrn_body(layer, seq_len, zero_init, *refs):
    if zero_init:
        za_ref, lbl_ref, ng_ref, o_ref, sf_ref = refs[:5]
        s0_ref = None
        scr = refs[5:]
    else:
        za_ref, lbl_ref, ng_ref, s0_ref, o_ref, sf_ref = refs[:6]
        scr = refs[6:]
    qs_refs, ke_refs, tot_refs, acc_refs, s_ref = scr[0:2], scr[2:4], scr[4:6], scr[6:8], scr[8]
    L = seq_len
    n_chunks = L // HG_CHUNK
    ones = _block_ones(DG, HD).astype(BF16)
    bd = _block_ones(DG, HD).astype(F32)

    lg = lbl_ref[...]
    e = jnp.exp(lg - jnp.max(lg, axis=0, keepdims=True))
    p = e / jnp.sum(e, axis=0, keepdims=True)
    lb = jnp.zeros((1, 2 * DG), F32)
    for m in range(1, layer + 1):
        lb = lb + p[m:m + 1, :]

    q = _silu(za_ref[:, 0:DG])
    v = za_ref[:, 3 * DG:4 * DG]
    j = _iota((L, DG), 0) % HG_CHUNK

    for d in range(2):
        lbd = lb[:, d * DG:(d + 1) * DG]
        f = lbd + (1.0 - lbd) * _sigmoid(za_ref[:, (1 + d) * DG:(2 + d) * DG])
        k = 1.0 - f
        lf = jnp.log(f)
        fc, rc = _chunk_cumsums(lf, HG_CHUNK)
        lam, rest = (fc, rc - lf) if d == 0 else (rc, fc - lf)
        qs_refs[d][...] = q * jnp.exp(lam)
        ke_refs[d][...] = k * jnp.exp(rest)
        tot_refs[d][...] = fc + rc - lf

        o = _head_sum(q * k, ones) * v
        for lag in range(1, HG_CHUNK):
            sgn = lag if d == 0 else -lag
            valid = (j >= lag) if d == 0 else (j < HG_CHUNK - lag)
            ex = jnp.exp(jnp.where(valid, lam - _shift(lam, sgn), NEG_BIG))
            o = o + _mm(q * _shift(k, sgn) * ex, ones) * _shift(v, sgn)
        acc_refs[d][...] = o
        if zero_init:
            s_ref[d] = jnp.zeros((DG, DG), F32)
        else:
            s_ref[d] = s0_ref[d]

    def chunk_step(c, carry):
        for d in range(2):
            ci = c if d == 0 else n_chunks - 1 - c
            r0 = pl.multiple_of(ci * HG_CHUNK, HG_CHUNK)
            sl = pl.ds(r0, HG_CHUNK)
            s = s_ref[d]
            acc_refs[d][sl, :] += _mm_nt(qs_refs[d][sl, :], s)
            dec = jnp.exp(tot_refs[d][pl.ds(r0, 1), :])
            upd = _mm_tn(za_ref[sl, 3 * DG:4 * DG], ke_refs[d][sl, :])
            s_ref[d] = s * dec + upd * bd
        return carry
    lax.fori_loop(0, n_chunks, chunk_step, 0, unroll=2)
    for d in range(2):
        sf_ref[d] = _fold_diag_blocks(s_ref[d])

    o = acc_refs[0][...] + acc_refs[1][...]
    ms = _head_sum(o * o, ones) * (1.0 / HD)
    o_ref[...] = o * lax.rsqrt(ms + EPS) * ng_ref[...] * _silu(za_ref[:, 4 * DG:5 * DG])


def _seq_specs(L, width, first_block):
    return pl.BlockSpec((L, width), lambda b: (first_block + b, 0))


def _lru_body(seq_len, zero_init, *refs):
    if zero_init:
        zb_ref, pv_ref, wa_ref, wx_ref, o_ref, hf_ref, a_ref, u_ref, acc_ref = refs
        h0_ref = None
    else:
        zb_ref, pv_ref, wa_ref, wx_ref, h0_ref, o_ref, hf_ref, a_ref, u_ref, acc_ref = refs
    L = seq_len
    n_chunks = L // SC_CHUNK
    rows = _iota((L, DG), 0)
    j = rows % SC_CHUNK
    xb = zb_ref[:, 0:DG]
    xc = (pv_ref[0:1, :] * jnp.where(rows >= 2, _shift(xb, 2), 0.0)
          + pv_ref[1:2, :] * jnp.where(rows >= 1, _shift(xb, 1), 0.0)
          + pv_ref[2:3, :] * xb
          + pv_ref[3:4, :] * jnp.where(rows < L - 1, _shift(xb, -1), 0.0)
          + pv_ref[4:5, :])
    acc_ref[...] = jnp.zeros((L, DG), F32)
    for d in range(2):
        ba, bx, lam = pv_ref[5 + d:6 + d, :], pv_ref[7 + d:8 + d, :], pv_ref[9 + d:10 + d, :]
        r = _sigmoid(_mm(xc, wa_ref[d]) + ba)
        i = _sigmoid(_mm(xc, wx_ref[d]) + bx)
        log_a = -LRU_C * r * _softplus(-lam)
        a = jnp.exp(log_a)
        u = jnp.sqrt(1.0 - jnp.exp(2.0 * log_a)) * (i * xc)
        s = 1
        while s < SC_CHUNK:
            sgn = s if d == 0 else -s
            valid = (j >= s) if d == 0 else (j < SC_CHUNK - s)
            u = u + a * jnp.where(valid, _shift(u, sgn), 0.0)
            a = a * jnp.where(valid, _shift(a, sgn), 1.0)
            s *= 2
        a_ref[...] = a
        u_ref[...] = u
        h0 = jnp.zeros((1, DG), F32) if zero_init else h0_ref[d:d + 1, :]

        def chunk_step(c, carry, d=d):
            ci = c if d == 0 else n_chunks - 1 - c
            r0 = pl.multiple_of(ci * SC_CHUNK, SC_CHUNK)
            h = u_ref[pl.ds(r0, SC_CHUNK), :] + a_ref[pl.ds(r0, SC_CHUNK), :] * carry
            acc_ref[pl.ds(r0, SC_CHUNK), :] += h
            return h[SC_CHUNK - 1:SC_CHUNK, :] if d == 0 else h[0:1, :]
        hf_ref[d:d + 1, :] = lax.fori_loop(0, n_chunks, chunk_step, h0)
    o_ref[...] = _gelu(zb_ref[:, DG:2 * DG]) * acc_ref[...]


def _batch_spec(shape):
    return pl.BlockSpec((None,) + shape, lambda b: (b,) + (0,) * len(shape))


def _mixer_call(body, name, z, z_width, params, state0, state_in_shape, state_out_shape, n_scratch_rows,
                extra_scratch):
    zero_init = state0 is None
    L, nseq = (CTX_LEN, N_CTX_SEQ) if zero_init else (LAT_LEN, N_LAT_SEQ)
    first = 0 if zero_init else T_CTX // LAT_LEN
    in_specs = [_seq_specs(L, z_width, first)]
    args = [z]
    for p in params:
        in_specs.append(pl.BlockSpec(p.shape, lambda b, nd=p.ndim: (0,) * nd))
        args.append(p)
    if not zero_init:
        in_specs.append(_batch_spec(state_in_shape))
        args.append(state0)
    return pl.pallas_call(
        functools.partial(body, L, zero_init),
        grid=(nseq,),
        in_specs=in_specs,
        out_specs=[_seq_specs(L, DG, 0), _batch_spec(state_out_shape)],
        out_shape=[jax.ShapeDtypeStruct((nseq * L, DG), F32),
                   jax.ShapeDtypeStruct((nseq,) + state_out_shape, F32)],
        scratch_shapes=[pltpu.VMEM((L, w), F32) for w in n_scratch_rows] + list(extra_scratch),
        compiler_params=_cparams(1), name=name + ("_ctx" if zero_init else "_lat"),
    )(*args)


RW_SCRATCH = 14


def _rwkv_body(seq_len, zero_init, *refs):
    if zero_init:
        zc_ref, pv_ref, wup_ref, aup_ref, gup_ref, o_ref, sf_ref = refs[:7]
        s0_ref = None
        scr = refs[7:]
    else:
        zc_ref, pv_ref, wup_ref, aup_ref, gup_ref, s0_ref, o_ref, sf_ref = refs[:8]
        scr = refs[8:]
    (kt_ref, rt_ref, kh_ref, bh_ref, kb_ref, bb_ref, tot_ref, v_ref, w_ref, u_ref, rh_ref, yh_ref,
     acc_ref, yf_ref, s_ref) = scr
    L = seq_len
    n_chunks = L // RW_CHUNK
    n_groups = L // RW_GROUP
    ones = _block_ones(DG, HD).astype(BF16)
    bd = _block_ones(DG, HD).astype(F32)
    rows = _iota((L, DG), 0)

    def pvr(i):
        return pv_ref[i:i + 1, :]

    def tshift(t, mu):
        prev = jnp.where(rows >= 1, _shift(t, 1), 0.0)
        nxt = jnp.where(rows < L - 1, _shift(t, -1), 0.0)
        return t + mu * (0.5 * (prev + nxt) - t)

    r = tshift(zc_ref[:, 0:DG], pvr(0))
    k = tshift(zc_ref[:, DG:2 * DG], pvr(1))
    v = tshift(zc_ref[:, 2 * DG:3 * DG], pvr(2))
    tail = zc_ref[:, 3 * DG:3 * DG + 128]
    kk = k * pvr(7)
    kk = kk / jnp.maximum(jnp.sqrt(_head_sum(kk * kk, ones)), 1e-12)
    gate = _mm(_sigmoid(tail), gup_ref[...])
    tanh_tail = jnp.tanh(tail)
    v_ref[...] = v
    acc_ref[...] = jnp.zeros((L, DG), F32)

    n_big = 4 * RW_GROUP
    ri, ci = _iota((n_big, n_big), 0), _iota((n_big, n_big), 1)
    same = ((ri // RW_GROUP) == (ci // RW_GROUP)) & (((ri % RW_GROUP) // RW_CHUNK) == ((ci % RW_GROUP) // RW_CHUNK))
    eye = (ri == ci).astype(F32)
    lane_head = _iota((RW_GROUP, DG), 1) // HD

    def pair_blocks(b):
        return ((ri // (2 * b)) == (ci // (2 * b))) & ((ri // b) != (ci // b))

    def stack_heads(x):
        return jnp.concatenate([jnp.where(lane_head == h, x, 0.0) for h in range(4)], axis=0)

    def fold_heads(x):
        return (x[0:RW_GROUP] + x[RW_GROUP:2 * RW_GROUP]
                + x[2 * RW_GROUP:3 * RW_GROUP] + x[3 * RW_GROUP:4 * RW_GROUP])

    for d in range(2):
        w_log = -_softplus(-(pvr(3 + d) + _mm(tanh_tail, wup_ref[d]))) - 0.5
        lw = -jnp.exp(w_log)
        a = _sigmoid(pvr(5 + d) + _mm(tail, aup_ref[d]))
        kd = k * (1.0 + (a - 1.0) * pvr(8))
        beta = kk * a
        acc_ref[...] += _head_sum(r * kd * pvr(9), ones) * v
        fc, rc = _chunk_cumsums(lw, RW_CHUNK)
        lam, rest = (fc, rc - lw) if d == 0 else (rc, fc - lw)
        kt_ref[...] = kk * jnp.exp(lam - lw)
        rt_ref[...] = r * jnp.exp(lam)
        inv = jnp.exp(-lam)
        kh_ref[...] = kd * inv
        bh_ref[...] = beta * inv
        ex_rest = jnp.exp(rest)
        kb_ref[...] = kd * ex_rest
        bb_ref[...] = beta * ex_rest
        tot_ref[...] = fc + rc - lw

        earlier = (ci < ri) if d == 0 else (ci > ri)
        m_strict = (same & earlier).astype(F32)
        m_incl = (same & (earlier | (ri == ci))).astype(F32)

        def local_step(g, carry):
            g0 = pl.multiple_of(g * RW_GROUP, RW_GROUP)
            sl = pl.ds(g0, RW_GROUP)
            lhs_k = stack_heads(kt_ref[sl, :])
            lhs_r = stack_heads(rt_ref[sl, :])
            rhs_b = jnp.concatenate([bh_ref[sl, :]] * 4, axis=0)
            rhs_k = jnp.concatenate([kh_ref[sl, :]] * 4, axis=0)
            v_big = stack_heads(v_ref[sl, :])
            n_mat = m_strict * _mm_nt(lhs_k, rhs_b)
            m_mat = m_strict * _mm_nt(lhs_k, rhs_k)
            a_rb = m_incl * _mm_nt(lhs_r, rhs_b)
            a_rk = m_incl * _mm_nt(lhs_r, rhs_k)
            t_mat = eye - jnp.where(pair_blocks(1), n_mat, 0.0)
            b = 2
            while b < RW_CHUNK:
                t_mat = t_mat - _mm(t_mat, _mm(jnp.where(pair_blocks(b), n_mat, 0.0), t_mat))
                b *= 2
            w_big = _mm(t_mat, lhs_k)
            u_big = _mm(t_mat, _mm(m_mat, v_big))
            w_ref[sl, :] = fold_heads(w_big)
            u_ref[sl, :] = fold_heads(u_big)
            rh_ref[sl, :] = fold_heads(lhs_r - _mm(a_rb, w_big))
            yh_ref[sl, :] = fold_heads(_mm(a_rk, v_big) - _mm(a_rb, u_big))
            return carry
        lax.fori_loop(0, n_groups, local_step, 0, unroll=2)

        if zero_init:
            s_ref[...] = jnp.zeros((DG, DG), F32)
        else:
            s_ref[...] = s0_ref[d]

        def chunk_step(c, carry, d=d):
            cidx = c if d == 0 else n_chunks - 1 - c
            r0 = pl.multiple_of(cidx * RW_CHUNK, RW_CHUNK)
            sl = pl.ds(r0, RW_CHUNK)
            s = s_ref[...]
            yh_ref[sl, :] = _mm_nt(rh_ref[sl, :], s) + yh_ref[sl, :]
            bb = bb_ref[sl, :].astype(BF16)
            phi = _mm_tn(w_ref[sl, :].astype(BF16), bb)
            delta = _mm_tn(jnp.concatenate([v_ref[sl, :], u_ref[sl, :]], axis=0).astype(BF16),
                           jnp.concatenate([kb_ref[sl, :].astype(BF16), -bb], axis=0))
            s_ref[...] = s * jnp.exp(tot_ref[pl.ds(r0, 1), :]) + bd * (delta - _mm(s, phi))
            return carry
        lax.fori_loop(0, n_chunks, chunk_step, 0, unroll=2)
        sf_ref[d] = _fold_diag_blocks(s_ref[...])
        if d == 0:
            yf_ref[...] = yh_ref[...]
    y = yf_ref[...] + yh_ref[...]
    mu = _head_sum(y, ones) * (1.0 / HD)
    yc = y - mu
    var = _head_sum(yc * yc, ones) * (1.0 / HD)
    y = yc * lax.rsqrt(var + C_LN_EPS) * pvr(10) + pvr(11)
    o_ref[...] = (y + acc_ref[...]) * gate


S5_BLK = 256


def _s5_body(seq_len, zero_init, *refs):
    if zero_init:
        zd_ref, pa_ref, pv_ref, bre_ref, bim_ref, cre_ref, cim_ref, glu_ref, o_ref, xf_ref = refs[:10]
        x0_ref = None
        scr = refs[10:]
    else:
        (zd_ref, pa_ref, pv_ref, bre_ref, bim_ref, cre_ref, cim_ref, glu_ref, x0_ref,
         o_ref, xf_ref) = refs[:11]
        scr = refs[11:]
    xr_ref, xi_ref, acc_ref = scr
    L = seq_len
    n_chunks = L // SC_CHUNK
    n_blk = L // S5_BLK
    jb = _iota((S5_BLK, S5_N), 0) % SC_CHUNK
    j8 = _iota((SC_CHUNK, S5_N), 0)

    def cmul(ar, ai, br, bi):
        return ar * br - ai * bi, ar * bi + ai * br

    for d in range(2):
        a_re, a_im, ls = pa_ref[d, 0:1, :], pa_ref[d, 1:2, :], pa_ref[d, 2:3, :]
        step = jnp.exp(ls)
        mag = jnp.exp(step * a_re)
        p1r, p1i = mag * jnp.cos(step * a_im), mag * jnp.sin(step * a_im)
        den = a_re * a_re + a_im * a_im
        zr = ((p1r - 1.0) * a_re + p1i * a_im) / den
        zi = (p1i * a_re - (p1r - 1.0) * a_im) / den
        p2r, p2i = cmul(p1r, p1i, p1r, p1i)
        p4r, p4i = cmul(p2r, p2i, p2r, p2i)
        powers = ((1, p1r, p1i), (2, p2r, p2i), (4, p4r, p4i))
        pwr = jnp.broadcast_to(p1r, (SC_CHUNK, S5_N))
        pwi = jnp.broadcast_to(p1i, (SC_CHUNK, S5_N))
        for s, _, _ in powers:
            sgn = s if d == 0 else -s
            valid = (j8 >= s) if d == 0 else (j8 < SC_CHUNK - s)
            nr, ni = cmul(pwr, pwi, _shift(pwr, sgn), _shift(pwi, sgn))
            pwr, pwi = jnp.where(valid, nr, pwr), jnp.where(valid, ni, pwi)

        def blk_step(bi, carry, d=d, zr=zr, zi=zi, powers=powers):
            sl = pl.ds(pl.multiple_of(bi * S5_BLK, S5_BLK), S5_BLK)
            u = zd_ref[sl, :]
            m1 = _mm(u, bre_ref[d])
            m2 = _mm(u, bim_ref[d])
            xr = zr * m1 - zi * m2
            xi = zr * m2 + zi * m1
            for s, pr, pi in powers:
                sgn = s if d == 0 else -s
                valid = (jb >= s) if d == 0 else (jb < SC_CHUNK - s)
                sr = jnp.where(valid, _shift(xr, sgn), 0.0)
                si = jnp.where(valid, _shift(xi, sgn), 0.0)
                xr, xi = xr + pr * sr - pi * si, xi + pr * si + pi * sr
            xr_ref[sl, :] = xr
            xi_ref[sl, :] = xi
            return carry
        lax.fori_loop(0, n_blk, blk_step, 0)

        if zero_init:
            c0 = (jnp.zeros((1, S5_N), F32), jnp.zeros((1, S5_N), F32))
        else:
            c0 = (x0_ref[d, 0:1, :], x0_ref[d, 1:2, :])

        def chunk_step(c, carry, d=d, pwr=pwr, pwi=pwi):
            ci = c if d == 0 else n_chunks - 1 - c
            sl = pl.ds(pl.multiple_of(ci * SC_CHUNK, SC_CHUNK), SC_CHUNK)
            cr, cim = carry
            hr = xr_ref[sl, :] + pwr * cr - pwi * cim
            hi = xi_ref[sl, :] + pwr * cim + pwi * cr
            xr_ref[sl, :] = hr
            xi_ref[sl, :] = hi
            last = SC_CHUNK - 1 if d == 0 else 0
            return hr[last:last + 1, :], hi[last:last + 1, :]
        fr, fi = lax.fori_loop(0, n_chunks, chunk_step, c0)
        xf_ref[d, 0:1, :] = fr
        xf_ref[d, 1:2, :] = fi

        def out_step(bi, carry, d=d):
            sl = pl.ds(pl.multiple_of(bi * S5_BLK, S5_BLK), S5_BLK)
            y = _mm(xr_ref[sl, :], cre_ref[d]) - _mm(xi_ref[sl, :], cim_ref[d])
            if d == 0:
                acc_ref[sl, :] = y
            else:
                acc_ref[sl, :] += y
            return carry
        lax.fori_loop(0, n_blk, out_step, 0)

    y = _gelu(acc_ref[...] + pv_ref[0:1, :] * zd_ref[...])
    o_ref[...] = y * _sigmoid(_mm(y, glu_ref[...]) + pv_ref[1:2, :])


def _outproj_body(*refs):
    ctx_refs, lat_refs = refs[0:4], refs[4:8]
    x_ref, mod_ref, mg_ref, w_ref, g2_ref, rw_ref, rb_ref, x1_ref, h2_ref, route_ref, wbf_ref = refs[8:]
    i = pl.program_id(0)

    @pl.when(i == 0)
    def _():
        wbf_ref[...] = w_ref[...].astype(BF16)

    r = _mod_row(i)
    is_ctx = i * TM < T_CTX
    y = jnp.zeros((TM, D_MODEL), F32)
    for n in range(4):
        on = _rms(jnp.where(is_ctx, ctx_refs[n][...], lat_refs[n][...])) * mg_ref[:, n * DG:(n + 1) * DG]
        y = y + jnp.dot(on.astype(BF16), wbf_ref[n * DG:(n + 1) * DG, :], preferred_element_type=F32)
    x1 = x_ref[...] + _mod_chunk(mod_ref, r, 2) * y
    x1_ref[...] = x1
    h2 = _rms(x1) * g2_ref[...] * (1.0 + _mod_chunk(mod_ref, r, 4)) + _mod_chunk(mod_ref, r, 3)
    h2_ref[...] = h2.astype(BF16)
    logits = jnp.dot(h2, rw_ref[...], precision=lax.Precision.HIGHEST,
                     preferred_element_type=F32) + rb_ref[...]
    lane = _iota((TM, N_EXPERTS), 1).astype(F32)
    col = _iota((TM, 2 * TOP_K), 1)
    work = logits
    top = jnp.max(work, axis=-1, keepdims=True)
    route = jnp.zeros((TM, 2 * TOP_K), F32)
    denom = jnp.zeros((TM, 1), F32)
    for k in range(TOP_K):
        m = jnp.max(work, axis=-1, keepdims=True)
        idx = jnp.min(jnp.where(work == m, lane, float(N_EXPERTS)), axis=-1, keepdims=True)
        pe = jnp.exp(m - top)
        route = route + jnp.where(col == k, pe, 0.0) + jnp.where(col == TOP_K + k, idx, 0.0)
        denom = denom + pe
        work = jnp.where(lane == idx, -jnp.inf, work)
    route_ref[...] = jnp.where(col < TOP_K, route / denom, route)


def _outproj(layer, o_ctx, o_lat, x, mod, mg, w_out, g2, router_w, rb):
    row = lambda i: (i, 0)
    full = lambda i: (0, 0)
    n_ctx_tiles = T_CTX // TM
    ctx_row = lambda i: (jnp.minimum(i, n_ctx_tiles - 1), 0)
    lat_row = lambda i: (jnp.maximum(i - n_ctx_tiles, 0), 0)
    return pl.pallas_call(
        _outproj_body,
        grid=(T_ALL // TM,),
        in_specs=[pl.BlockSpec((TM, DG), ctx_row)] * 4 + [pl.BlockSpec((TM, DG), lat_row)] * 4 + [
            pl.BlockSpec((TM, D_MODEL), row),
            _mod_spec(layer),
            pl.BlockSpec((1, D_MODEL), full),
            pl.BlockSpec((None, D_MODEL, D_MODEL), lambda i: (layer, 0, 0)),
            pl.BlockSpec((1, D_MODEL), full),
            pl.BlockSpec((None, D_MODEL, N_EXPERTS), lambda i: (layer, 0, 0)),
            pl.BlockSpec((1, N_EXPERTS), full)],
        out_specs=[pl.BlockSpec((TM, D_MODEL), row), pl.BlockSpec((TM, D_MODEL), row),
                   pl.BlockSpec((TM, 2 * TOP_K), row)],
        out_shape=[jax.ShapeDtypeStruct((T_ALL, D_MODEL), F32),
                   jax.ShapeDtypeStruct((T_ALL, D_MODEL), BF16),
                   jax.ShapeDtypeStruct((T_ALL, 2 * TOP_K), F32)],
        scratch_shapes=[pltpu.VMEM((D_MODEL, D_MODEL), BF16)],
        compiler_params=_cparams(1), name="outproj_router",
    )(*o_ctx, *o_lat, x, mod, mg.reshape(1, D_MODEL), w_out, g2.reshape(1, D_MODEL), router_w,
      rb.reshape(1, N_EXPERTS))


def _route_tables(route):
    prob = route[:, :TOP_K].reshape(N_PARTS, MOE_SLOTS)
    eid = route[:, TOP_K:].astype(jnp.int32).reshape(N_PARTS, MOE_SLOTS)
    slot = jnp.arange(MOE_SLOTS, dtype=jnp.int32)
    experts = jnp.arange(N_EXPERTS, dtype=jnp.int32)
    cnt = jnp.sum((eid[:, :, None] == experts[None, None, :]).astype(jnp.int32), axis=1)
    zero = jnp.zeros((N_PARTS, 1), jnp.int32)
    toffs = jnp.concatenate([zero, jnp.cumsum((cnt + MOE_TILE - 1) // MOE_TILE, axis=1)], axis=1)
    stride = MOE_SLOTS + MOE_TILE
    last = N_EXPERTS * stride
    n_dummy = MOE_TILE - 1
    dummy_j = jnp.arange(n_dummy, dtype=jnp.int32)
    need = (-cnt) % MOE_TILE
    dummy_key = jnp.where(dummy_j[None, None, :] < need[:, :, None],
                          experts[None, :, None] * stride + MOE_SLOTS + dummy_j[None, None, :], last)
    n_fill = MOE_TILES * MOE_TILE - MOE_SLOTS - N_EXPERTS * n_dummy
    keys = jnp.concatenate([eid * stride + slot[None, :], dummy_key.reshape(N_PARTS, N_EXPERTS * n_dummy),
                            jnp.full((N_PARTS, n_fill), last, jnp.int32)], axis=1)
    n_pad = MOE_TILES * MOE_TILE - MOE_SLOTS
    wgt = jnp.concatenate([prob, jnp.zeros((N_PARTS, n_pad), F32)], axis=1)
    keys, wgt = lax.sort((keys, wgt), dimension=1, num_keys=1)
    in_slot = keys % stride
    tok = jnp.where((in_slot < MOE_SLOTS) & (keys < last), in_slot // TOP_K, -1)
    return (toffs.reshape(-1), tok.reshape(N_PARTS * MOE_TILES, MOE_TILE),
            wgt.reshape(N_PARTS * MOE_TILES, MOE_TILE))


def _moe_body(toffs_ref, h_ref, tok_ref, wgt_ref, wgu_ref, bgu_ref, wdn_ref, bdn_ref, o_ref):
    p, e = pl.program_id(0), pl.program_id(1)

    @pl.when(e == 0)
    def _():
        o_ref[...] = jnp.zeros((MOE_PART, D_MODEL), F32)

    ones = jnp.ones((MOE_TILE, MOE_TILE), BF16)
    eye = _iota((MOE_TILE, MOE_TILE), 0) == _iota((MOE_TILE, MOE_TILE), 1)
    lane_tok = _iota((MOE_TILE, MOE_PART), 1)
    lane_hi = (lane_tok // 64).astype(F32)
    lane_lo = (lane_tok % 64).astype(F32)
    row_tok = _iota((MOE_PART, MOE_TILE), 0)
    row_hi = (row_tok // 64).astype(F32)
    row_lo = (row_tok % 64).astype(F32)

    def to_column(row):
        return jnp.dot(jnp.where(eye, row, 0.0).astype(BF16), ones, preferred_element_type=F32)

    def tile_step(ti, carry):
        tokr = tok_ref[pl.ds(ti, 1), :].astype(F32)
        hi = jnp.floor(tokr * (1.0 / 64.0))
        lo = tokr - 64.0 * hi
        hi_c = jnp.concatenate([to_column(hi)] * (MOE_PART // MOE_TILE), axis=1)
        lo_c = jnp.concatenate([to_column(lo)] * (MOE_PART // MOE_TILE), axis=1)
        gather = ((lane_hi == hi_c) & (lane_lo == lo_c)).astype(BF16)
        xs = jnp.dot(gather, h_ref[...], preferred_element_type=F32).astype(BF16)
        gu = jnp.dot(xs, wgu_ref[...], preferred_element_type=F32) + bgu_ref[...]
        gate = jnp.minimum(gu[:, :D_FF], SWIGLU_LIMIT)
        up = jnp.clip(gu[:, D_FF:], -SWIGLU_LIMIT, SWIGLU_LIMIT)
        act = (up + 1.0) * gate * _sigmoid(SWIGLU_ALPHA * gate)
        ys = jnp.dot(act.astype(BF16), wdn_ref[...], preferred_element_type=F32) + bdn_ref[...]
        wr = wgt_ref[pl.ds(ti, 1), :]
        w_hi = wr.astype(BF16).astype(F32)
        w_col = to_column(w_hi) + to_column(wr - w_hi)
        ysw = (ys * jnp.concatenate([w_col] * (D_MODEL // MOE_TILE), axis=1)).astype(BF16)
        scatter = ((row_hi == hi) & (row_lo == lo)).astype(BF16)
        for c in range(MOE_PART // 512):
            o_ref[c * 512:(c + 1) * 512, :] += jnp.dot(scatter[c * 512:(c + 1) * 512, :], ysw,
                                                      preferred_element_type=F32)
        return carry

    lax.fori_loop(toffs_ref[p * (N_EXPERTS + 1) + e], toffs_ref[p * (N_EXPERTS + 1) + e + 1], tile_step, 0)


def _moe(layer, h2, toffs, tok, wgt, w_gu, b_gu, w_dn, b_dn):
    grid_spec = pltpu.PrefetchScalarGridSpec(
        num_scalar_prefetch=1,
        grid=(N_PARTS, N_EXPERTS),
        in_specs=[pl.BlockSpec((MOE_PART, D_MODEL), lambda p, e, t: (p, 0)),
                  pl.BlockSpec((MOE_TILES, MOE_TILE), lambda p, e, t: (p, 0)),
                  pl.BlockSpec((MOE_TILES, MOE_TILE), lambda p, e, t: (p, 0)),
                  pl.BlockSpec((None, None, D_MODEL, 2 * D_FF), lambda p, e, t: (layer, e, 0, 0)),
                  pl.BlockSpec((None, None, 1, 2 * D_FF), lambda p, e, t: (layer, e, 0, 0)),
                  pl.BlockSpec((None, None, D_FF, D_MODEL), lambda p, e, t: (layer, e, 0, 0)),
                  pl.BlockSpec((None, None, 1, D_MODEL), lambda p, e, t: (layer, e, 0, 0))],
        out_specs=pl.BlockSpec((MOE_PART, D_MODEL), lambda p, e, t: (p, 0)),
    )
    return pl.pallas_call(
        _moe_body,
        grid_spec=grid_spec,
        out_shape=jax.ShapeDtypeStruct((T_ALL, D_MODEL), F32),
        compiler_params=_cparams(2), name="moe",
    )(toffs, h2, tok, wgt, w_gu, b_gu.reshape(DEPTH, N_EXPERTS, 1, 2 * D_FF), w_dn,
      b_dn.reshape(DEPTH, N_EXPERTS, 1, D_MODEL))


def _final_body(x1_ref, m_ref, mod_ref, g_ref, o_ref):
    r = _mod_row(pl.program_id(0))
    o_ref[...] = _rms(x1_ref[...] + _mod_chunk(mod_ref, r, 5) * m_ref[...]) * g_ref[...]


def _final_norm(x1, moe_out, mod, g):
    tok_spec = pl.BlockSpec((TM, D_MODEL), lambda i: (i, 0))
    return pl.pallas_call(
        _final_body,
        grid=(T_ALL // TM,),
        in_specs=[tok_spec, tok_spec, _mod_spec(DEPTH - 1), pl.BlockSpec((1, D_MODEL), lambda i: (0, 0))],
        out_specs=tok_spec,
        out_shape=jax.ShapeDtypeStruct((T_ALL, D_MODEL), F32),
        compiler_params=_cparams(1), name="final_norm",
    )(x1, moe_out, mod, g.reshape(1, D_MODEL))


def _to_block_diag(s):
    eye = jnp.eye(4, dtype=s.dtype)
    out = jnp.einsum('hg,...hab->...hagb', eye, s)
    return out.reshape(s.shape[:-3] + (DG, DG))


def _pad_rows(w, first, total=128):
    pad = [(0, 0)] * (w.ndim - 2) + [(first, total - first - w.shape[-2]), (0, 0)]
    return jnp.pad(w, pad)


def kernel(x_prompt, x_sample, c, state_hgrn, state_lru, state_rwkv, state_s5, c_ctx, ada_w, ada_b, norm1_g, norm2_g, w_in, w_out, merge_g, hgrn_lb_logits, hgrn_norm_g, lru_conv_w, lru_conv_b, lru_wa, lru_ba, lru_wx, lru_bx, lru_lambda, rwkv_mu, rwkv_w0, rwkv_w_up, rwkv_a0, rwkv_a_up, rwkv_g_up, rwkv_k_k, rwkv_k_a, rwkv_r_k, rwkv_ln_g, rwkv_ln_b, s5_a_re, s5_a_im, s5_log_step, s5_b_re, s5_b_im, s5_c_re, s5_c_im, s5_d, s5_glu_w, s5_glu_b, router_w, router_b, moe_w_gu, moe_b_gu, moe_w_down, moe_b_down, final_norm_g):
    cvec = jnp.concatenate([c_ctx[None], c, jnp.zeros((5, D_MODEL), F32)], axis=0)
    x_cat = jnp.concatenate([x_prompt.reshape(T_CTX, D_MODEL),
                             x_sample.reshape(N_LAT_SEQ * LAT_LEN, D_MODEL)], axis=0)
    lbl = hgrn_lb_logits.reshape(DEPTH, 2 * DG)
    hg_s0 = _to_block_diag(jnp.swapaxes(state_hgrn, -1, -2))
    rw_s0 = _to_block_diag(state_rwkv)
    s5_x0 = jnp.moveaxis(state_s5.reshape(N_LAT_SEQ, DEPTH, 2, S5_N, 2), -1, -2)
    lru_pv = jnp.concatenate([lru_conv_w, lru_conv_b[:, None], lru_ba, lru_bx, lru_lambda,
                              jnp.zeros((DEPTH, 5, DG), F32)], axis=1)
    lru_wa_bd = _to_block_diag(lru_wa)
    lru_wx_bd = _to_block_diag(lru_wx)
    rw_pv = jnp.concatenate([rwkv_mu, rwkv_w0, rwkv_a0, rwkv_k_k[:, None], rwkv_k_a[:, None],
                             rwkv_r_k[:, None], rwkv_ln_g[:, None], rwkv_ln_b[:, None],
                             jnp.zeros((DEPTH, 4, DG), F32)], axis=1)
    rw_wup = _pad_rows(rwkv_w_up, 0)
    rw_aup = _pad_rows(rwkv_a_up, 32)
    rw_gup = _pad_rows(rwkv_g_up, 64)
    s5_pa = jnp.stack([s5_a_re.reshape(DEPTH, 2, S5_N), s5_a_im.reshape(DEPTH, 2, S5_N),
                       jnp.repeat(s5_log_step, 64, axis=-1)], axis=2)
    s5_pa = jnp.pad(s5_pa, ((0, 0), (0, 0), (0, 5), (0, 0)))
    s5_pv = jnp.concatenate([s5_d[:, None], s5_glu_b[:, None], jnp.zeros((DEPTH, 6, DG), F32)], axis=1)
    eye16 = jnp.eye(16, dtype=F32)
    s5_bre = jnp.einsum('gh,ldgpc->ldgchp', eye16, s5_b_re).reshape(DEPTH, 2, DG, S5_N)
    s5_bim = jnp.einsum('gh,ldgpc->ldgchp', eye16, s5_b_im).reshape(DEPTH, 2, DG, S5_N)
    s5_cre = jnp.einsum('gh,ldgcp->ldgphc', eye16, s5_c_re).reshape(DEPTH, 2, S5_N, DG)
    s5_cim = jnp.einsum('gh,ldgcp->ldgphc', eye16, s5_c_im).reshape(DEPTH, 2, S5_N, DG)

    mod = _ada_table(cvec, ada_w, ada_b)
    x1 = _embed(x_cat, _grid_pos_table())
    moe_out = None
    w_gu_bf = moe_w_gu.astype(BF16)
    w_dn_bf = moe_w_down.astype(BF16)

    new_hgrn, new_lru, new_rwkv, new_s5 = [], [], [], []
    for l in range(DEPTH):
        x, za, zb, zc, zd = _inproj(l, x1, moe_out, mod, norm1_g[l], w_in)

        hg_body = functools.partial(_hgrn_body, l)
        hg_params = (lbl, hgrn_norm_g[l].reshape(1, DG))
        hg_extra = (pltpu.VMEM((2, DG, DG), F32),)
        hg_call = functools.partial(_mixer_call, hg_body, "hgrn", za, A_COLS, hg_params)
        hg_tail = ((2, DG, DG), (2, HD, DG), (DG,) * HG_SCRATCH, hg_extra)
        oa, sa = hg_call(None, *hg_tail)
        oa_lat, _ = hg_call(hg_s0[:, l], *hg_tail)

        lru_params = (lru_pv[l], lru_wa_bd[l], lru_wx_bd[l])
        lru_call = functools.partial(_mixer_call, _lru_body, "lru", zb, B_COLS, lru_params)
        lru_tail = ((2, DG), (2, DG), (DG, DG, DG), ())
        ob, sb = lru_call(None, *lru_tail)
        ob_lat, _ = lru_call(state_lru[:, l], *lru_tail)

        rw_params = (rw_pv[l], rw_wup[l], rw_aup[l], rw_gup[l])
        rw_call = functools.partial(_mixer_call, _rwkv_body, "rwkv", zc, C_COLS, rw_params)
        rw_tail = ((2, DG, DG), (2, HD, DG), (DG,) * RW_SCRATCH, (pltpu.VMEM((DG, DG), F32),))
        oc, sc = rw_call(None, *rw_tail)
        oc_lat, _ = rw_call(rw_s0[:, l], *rw_tail)

        s5_params = (s5_pa[l], s5_pv[l], s5_bre[l], s5_bim[l], s5_cre[l], s5_cim[l], s5_glu_w[l])
        s5_call = functools.partial(_mixer_call, _s5_body, "s5", zd, D_COLS, s5_params)
        s5_tail = ((2, 2, S5_N), (2, 2, S5_N), (S5_N, S5_N, DG), ())
        od, sd = s5_call(None, *s5_tail)
        od_lat, _ = s5_call(s5_x0[:, l], *s5_tail)

        x1, h2, route = _outproj(l, (oa, ob, oc, od), (oa_lat, ob_lat, oc_lat, od_lat), x, mod, merge_g[l],
                                 w_out, norm2_g[l], router_w, router_b[l])
        toffs, tok, wgt = _route_tables(route)
        moe_out = _moe(l, h2, toffs, tok, wgt, w_gu_bf, moe_b_gu, w_dn_bf, moe_b_down)
        new_hgrn.append(sa)
        new_lru.append(sb)
        new_rwkv.append(sc)
        new_s5.append(sd)

    y = _final_norm(x1, moe_out, mod, final_norm_g)
    y_prompt = y[:T_CTX].reshape(N_CTX_SEQ, CTX_LEN, D_MODEL)
    y_sample = y[T_CTX:].reshape(N_LAT_SEQ, LAT_LEN, D_MODEL)
    st_shape = (N_CTX_SEQ, DEPTH, 2, HD, 4, HD)
    st_hgrn = jnp.transpose(jnp.stack(new_hgrn, axis=1).reshape(st_shape), (0, 1, 2, 4, 5, 3))
    st_lru = jnp.stack(new_lru, axis=1)
    st_rwkv = jnp.transpose(jnp.stack(new_rwkv, axis=1).reshape(st_shape), (0, 1, 2, 4, 3, 5))
    st_s5 = jnp.moveaxis(jnp.stack(new_s5, axis=1), -2, -1).reshape(N_CTX_SEQ, DEPTH, 2, 16, 64, 2)
    return (y_prompt, y_sample, st_hgrn, st_lru, st_rwkv, st_s5)
```

```python
import functools
import math

import jax
import jax.numpy as jnp
from jax import lax
from jax.experimental import pallas as pl
from jax.experimental.pallas import tpu as pltpu

F32 = jnp.float32
BF16 = jnp.bfloat16

D_MODEL = 1024
DEPTH = 4
N_CTX_SEQ, CTX_LEN = 16, 256
N_LAT_SEQ, LAT_LEN = 2, 1024
T_CTX = N_CTX_SEQ * CTX_LEN
T_ALL = T_CTX + N_LAT_SEQ * LAT_LEN
GRID_W = 64
DG = 256
HD = 64
EPS = 1e-6
A_COLS, B_COLS, C_COLS, D_COLS = 5 * DG, 2 * DG, 3 * DG + 128, DG
N_IN = A_COLS + B_COLS + C_COLS + D_COLS
HG_CHUNK = 8
RW_CHUNK = 64
RW_GROUP = 64
SC_CHUNK = 8
LRU_C = 8.0
C_LN_EPS = 64e-5
S5_N = 1024
N_EXPERTS = 32
TOP_K = 4
D_FF = 1024
SWIGLU_ALPHA = 1.702
SWIGLU_LIMIT = 7.0
TM = 256
MOE_PART = 1536
N_PARTS = T_ALL // MOE_PART
MOE_SLOTS = MOE_PART * TOP_K
MOE_TILE = 256
MOE_TILES = MOE_SLOTS // MOE_TILE + N_EXPERTS
VMEM_LIMIT = 56 * 1024 * 1024
NEG_BIG = -1e30


def _cparams(n_axes):
    return pltpu.CompilerParams(dimension_semantics=("arbitrary",) * n_axes,
                                vmem_limit_bytes=VMEM_LIMIT)


def _iota(shape, axis):
    return lax.broadcasted_iota(jnp.int32, shape, axis)


def _shift(x, d):
    n = x.shape[0]
    d = d % n
    return x if d == 0 else pltpu.roll(x, d, 0)


def _mm(a, b):
    return jnp.dot(a.astype(BF16), b.astype(BF16), preferred_element_type=F32)


def _mm_nt(a, b):
    return lax.dot_general(a.astype(BF16), b.astype(BF16), (((1,), (1,)), ((), ())),
                           preferred_element_type=F32)


def _mm_tn(a, b):
    return lax.dot_general(a, b, (((0,), (0,)), ((), ())), preferred_element_type=F32)


def _block_ones(n, blk):
    r = _iota((n, n), 0) // blk
    c = _iota((n, n), 1) // blk
    return r == c


def _fold_diag_blocks(s):
    return s[0:HD] + s[HD:2 * HD] + s[2 * HD:3 * HD] + s[3 * HD:4 * HD]


def _head_sum(x, ones_bf16):
    hi = x.astype(BF16)
    lo = (x - hi.astype(F32)).astype(BF16)
    return (jnp.dot(hi, ones_bf16, preferred_element_type=F32)
            + jnp.dot(lo, ones_bf16, preferred_element_type=F32))


def _sigmoid(x):
    return 1.0 / (1.0 + jnp.exp(-x))


def _silu(x):
    return x * _sigmoid(x)


def _softplus(x):
    return jnp.maximum(x, 0.0) + jnp.log(1.0 + jnp.exp(-jnp.abs(x)))


def _gelu(x):
    c = math.sqrt(2.0 / math.pi)
    return 0.5 * x * (1.0 + jnp.tanh(c * (x + 0.044715 * (x * x * x))))


def _rms(x, eps=EPS):
    return x * lax.rsqrt(jnp.mean(x * x, axis=-1, keepdims=True) + eps)


def _chunk_cumsums(x, chunk):
    j = _iota(x.shape, 0) % chunk
    f, r = x, x
    s = 1
    while s < chunk:
        f = f + jnp.where(j >= s, _shift(f, s), 0.0)
        r = r + jnp.where(j < chunk - s, _shift(r, -s), 0.0)
        s *= 2
    return f, r


def _mod_row(i):
    start = i * TM
    return jnp.where(start < T_CTX, 0, 1 + (start - T_CTX) // LAT_LEN)


def _ada_body(c_ref, w_ref, b_ref, o_ref):
    o_ref[...] = _mm(_silu(c_ref[...]), w_ref[...]) + b_ref[...]


def _ada_table(cvec, ada_w, ada_b):
    nb = 6 * D_MODEL // 1024
    return pl.pallas_call(
        _ada_body,
        grid=(DEPTH, nb),
        in_specs=[pl.BlockSpec((8, D_MODEL), lambda l, j: (0, 0)),
                  pl.BlockSpec((None, D_MODEL, 1024), lambda l, j: (l, 0, j)),
                  pl.BlockSpec((None, 1, 1024), lambda l, j: (l, 0, j))],
        out_specs=pl.BlockSpec((None, 8, 1024), lambda l, j: (l, 0, j)),
        out_shape=jax.ShapeDtypeStruct((DEPTH, 8, 6 * D_MODEL), F32),
        compiler_params=_cparams(2), name="ada_table",
    )(cvec, ada_w, ada_b.reshape(DEPTH, 1, 6 * D_MODEL))


def _embed_body(x_ref, p_ref, o_ref):
    i = pl.program_id(0)

    @pl.when(i * TM < T_CTX)
    def _():
        o_ref[...] = x_ref[...]

    @pl.when(i * TM >= T_CTX)
    def _():
        o_ref[...] = x_ref[...] + p_ref[...]


def _embed(x_cat, pos):
    per_seq = LAT_LEN // TM
    first = T_CTX // TM
    return pl.pallas_call(
        _embed_body,
        grid=(T_ALL // TM,),
        in_specs=[pl.BlockSpec((TM, D_MODEL), lambda i: (i, 0)),
                  pl.BlockSpec((TM, D_MODEL), lambda i: (jnp.maximum(i - first, 0) % per_seq, 0))],
        out_specs=pl.BlockSpec((TM, D_MODEL), lambda i: (i, 0)),
        out_shape=jax.ShapeDtypeStruct((T_ALL, D_MODEL), F32),
        compiler_params=_cparams(1), name="embed",
    )(x_cat, pos)


def _grid_pos_table():
    t = jnp.arange(LAT_LEN)
    row = (t // GRID_W).astype(F32)
    col = (t % GRID_W).astype(F32)
    quarter = D_MODEL // 4
    freq = jnp.exp(-math.log(10000.0) * jnp.arange(quarter, dtype=F32) / quarter)

    def enc(p):
        ang = p[:, None] * freq[None, :]
        return jnp.concatenate([jnp.sin(ang), jnp.cos(ang)], axis=-1)
    return jnp.concatenate([enc(row), enc(col)], axis=-1)


def _mod_chunk(mod_ref, row, n):
    return mod_ref[pl.ds(row, 1), n * D_MODEL:(n + 1) * D_MODEL]


def _mod_spec(layer):
    return pl.BlockSpec((None, 8, 6 * D_MODEL), lambda i: (layer, 0, 0))


def _inproj_body(has_moe, *refs):
    if has_moe:
        (x1_ref, m_ref, pmod_ref, mod_ref, g_ref, w_ref,
         x_ref, za_ref, zb_ref, zc_ref, zd_ref, wbf_ref) = refs
    else:
        x_ref, mod_ref, g_ref, w_ref, za_ref, zb_ref, zc_ref, zd_ref, wbf_ref = refs
    i = pl.program_id(0)

    @pl.when(i == 0)
    def _():
        wbf_ref[...] = w_ref[...].astype(BF16)

    r = _mod_row(i)
    if has_moe:
        x = x1_ref[...] + _mod_chunk(pmod_ref, r, 5) * m_ref[...]
        x_ref[...] = x
    else:
        x = x_ref[...]
    h = _rms(x) * g_ref[...] * (1.0 + _mod_chunk(mod_ref, r, 1)) + _mod_chunk(mod_ref, r, 0)
    z = jnp.dot(h.astype(BF16), wbf_ref[...], preferred_element_type=F32)
    za_ref[...] = z[:, 0:A_COLS]
    zb_ref[...] = z[:, A_COLS:A_COLS + B_COLS]
    zc_ref[...] = z[:, A_COLS + B_COLS:A_COLS + B_COLS + C_COLS]
    zd_ref[...] = z[:, A_COLS + B_COLS + C_COLS:N_IN]


def _inproj(layer, x, moe_out, mod, g, w_in):
    has_moe = moe_out is not None
    row = lambda i: (i, 0)
    widths = (A_COLS, B_COLS, C_COLS, D_COLS)
    tok_spec = pl.BlockSpec((TM, D_MODEL), row)
    in_specs = [tok_spec]
    args = [x]
    if has_moe:
        in_specs += [tok_spec, _mod_spec(layer - 1)]
        args += [moe_out, mod]
    in_specs += [_mod_spec(layer), pl.BlockSpec((1, D_MODEL), lambda i: (0, 0)),
                 pl.BlockSpec((None, D_MODEL, N_IN), lambda i: (layer, 0, 0))]
    args += [mod, g.reshape(1, D_MODEL), w_in]
    out_specs = [pl.BlockSpec((TM, wd), row) for wd in widths]
    out_shape = [jax.ShapeDtypeStruct((T_ALL, wd), F32) for wd in widths]
    if has_moe:
        out_specs = [tok_spec] + out_specs
        out_shape = [jax.ShapeDtypeStruct((T_ALL, D_MODEL), F32)] + out_shape
    outs = pl.pallas_call(
        functools.partial(_inproj_body, has_moe),
        grid=(T_ALL // TM,),
        in_specs=in_specs, out_specs=out_specs, out_shape=out_shape,
        scratch_shapes=[pltpu.VMEM((D_MODEL, N_IN), BF16)],
        compiler_params=_cparams(1), name="inproj",
    )(*args)
    return list(outs) if has_moe else [x] + list(outs)


HG_SCRATCH = 8


def _hgrn_body(layer, seq_len, zero_init, *refs):
    if zero_init:
        za_ref, lbl_ref, ng_ref, o_ref, sf_ref = refs[:5]
        s0_ref = None
        scr = refs[5:]
    else:
        za_ref, lbl_ref, ng_ref, s0_ref, o_ref, sf_ref = refs[:6]
        scr = refs[6:]
    qs_refs, ke_refs, tot_refs, acc_refs, s_ref = scr[0:2], scr[2:4], scr[4:6], scr[6:8], scr[8]
    L = seq_len
    n_chunks = L // HG_CHUNK
    ones = _block_ones(DG, HD).astype(BF16)
    bd = _block_ones(DG, HD).astype(F32)

    lg = lbl_ref[...]
    e = jnp.exp(lg - jnp.max(lg, axis=0, keepdims=True))
    p = e / jnp.sum(e, axis=0, keepdims=True)
    lb = jnp.zeros((1, 2 * DG), F32)
    for m in range(1, layer + 1):
        lb = lb + p[m:m + 1, :]

    q = _silu(za_ref[:, 0:DG])
    v = za_ref[:, 3 * DG:4 * DG]
    j = _iota((L, DG), 0) % HG_CHUNK

    for d in range(2):
        lbd = lb[:, d * DG:(d + 1) * DG]
        f = lbd + (1.0 - lbd) * _sigmoid(za_ref[:, (1 + d) * DG:(2 + d) * DG])
        k = 1.0 - f
        lf = jnp.log(f)
        fc, rc = _chunk_cumsums(lf, HG_CHUNK)
        lam, rest = (fc, rc - lf) if d == 0 else (rc, fc - lf)
        qs_refs[d][...] = q * jnp.exp(lam)
        ke_refs[d][...] = k * jnp.exp(rest)
        tot_refs[d][...] = fc + rc - lf

        o = _head_sum(q * k, ones) * v
        for lag in range(1, HG_CHUNK):
            sgn = lag if d == 0 else -lag
            valid = (j >= lag) if d == 0 else (j < HG_CHUNK - lag)
            ex = jnp.exp(jnp.where(valid, lam - _shift(lam, sgn), NEG_BIG))
            o = o + _mm(q * _shift(k, sgn) * ex, ones) * _shift(v, sgn)
        acc_refs[d][...] = o
        if zero_init:
            s_ref[d] = jnp.zeros((DG, DG), F32)
        else:
            s_ref[d] = s0_ref[d]

    def chunk_step(c, carry):
        for d in range(2):
            ci = c if d == 0 else n_chunks - 1 - c
            r0 = pl.multiple_of(ci * HG_CHUNK, HG_CHUNK)
            sl = pl.ds(r0, HG_CHUNK)
            s = s_ref[d]
            acc_refs[d][sl, :] += _mm_nt(qs_refs[d][sl, :], s)
            dec = jnp.exp(tot_refs[d][pl.ds(r0, 1), :])
            upd = _mm_tn(za_ref[sl, 3 * DG:4 * DG], ke_refs[d][sl, :])
            s_ref[d] = s * dec + upd * bd
        return carry
    lax.fori_loop(0, n_chunks, chunk_step, 0, unroll=2)
    for d in range(2):
        sf_ref[d] = _fold_diag_blocks(s_ref[d])

    o = acc_refs[0][...] + acc_refs[1][...]
    ms = _head_sum(o * o, ones) * (1.0 / HD)
    o_ref[...] = o * lax.rsqrt(ms + EPS) * ng_ref[...] * _silu(za_ref[:, 4 * DG:5 * DG])


def _seq_specs(L, width, first_block):
    return pl.BlockSpec((L, width), lambda b: (first_block + b, 0))


def _lru_body(seq_len, zero_init, *refs):
    if zero_init:
        zb_ref, pv_ref, wa_ref, wx_ref, o_ref, hf_ref, a_ref, u_ref, acc_ref = refs
        h0_ref = None
    else:
        zb_ref, pv_ref, wa_ref, wx_ref, h0_ref, o_ref, hf_ref, a_ref, u_ref, acc_ref = refs
    L = seq_len
    n_chunks = L // SC_CHUNK
    rows = _iota((L, DG), 0)
    j = rows % SC_CHUNK
    xb = zb_ref[:, 0:DG]
    xc = (pv_ref[0:1, :] * jnp.where(rows >= 2, _shift(xb, 2), 0.0)
          + pv_ref[1:2, :] * jnp.where(rows >= 1, _shift(xb, 1), 0.0)
          + pv_ref[2:3, :] * xb
          + pv_ref[3:4, :] * jnp.where(rows < L - 1, _shift(xb, -1), 0.0)
          + pv_ref[4:5, :])
    acc_ref[...] = jnp.zeros((L, DG), F32)
    for d in range(2):
        ba, bx, lam = pv_ref[5 + d:6 + d, :], pv_ref[7 + d:8 + d, :], pv_ref[9 + d:10 + d, :]
        r = _sigmoid(_mm(xc, wa_ref[d]) + ba)
        i = _sigmoid(_mm(xc, wx_ref[d]) + bx)
        log_a = -LRU_C * r * _softplus(-lam)
        a = jnp.exp(log_a)
        u = jnp.sqrt(1.0 - jnp.exp(2.0 * log_a)) * (i * xc)
        s = 1
        while s < SC_CHUNK:
            sgn = s if d == 0 else -s
            valid = (j >= s) if d == 0 else (j < SC_CHUNK - s)
            u = u + a * jnp.where(valid, _shift(u, sgn), 0.0)
            a = a * jnp.where(valid, _shift(a, sgn), 1.0)
            s *= 2
        a_ref[...] = a
        u_ref[...] = u
        h0 = jnp.zeros((1, DG), F32) if zero_init else h0_ref[d:d + 1, :]

        def chunk_step(c, carry, d=d):
            ci = c if d == 0 else n_chunks - 1 - c
            r0 = pl.multiple_of(ci * SC_CHUNK, SC_CHUNK)
            h = u_ref[pl.ds(r0, SC_CHUNK), :] + a_ref[pl.ds(r0, SC_CHUNK), :] * carry
            acc_ref[pl.ds(r0, SC_CHUNK), :] += h
            return h[SC_CHUNK - 1:SC_CHUNK, :] if d == 0 else h[0:1, :]
        hf_ref[d:d + 1, :] = lax.fori_loop(0, n_chunks, chunk_step, h0)
    o_ref[...] = _gelu(zb_ref[:, DG:2 * DG]) * acc_ref[...]


def _batch_spec(shape):
    return pl.BlockSpec((None,) + shape, lambda b: (b,) + (0,) * len(shape))


def _mixer_call(body, name, z, z_width, params, state0, state_in_shape, state_out_shape, n_scratch_rows,
                extra_scratch):
    zero_init = state0 is None
    L, nseq = (CTX_LEN, N_CTX_SEQ) if zero_init else (LAT_LEN, N_LAT_SEQ)
    first = 0 if zero_init else T_CTX // LAT_LEN
    in_specs = [_seq_specs(L, z_width, first)]
    args = [z]
    for p in params:
        in_specs.append(pl.BlockSpec(p.shape, lambda b, nd=p.ndim: (0,) * nd))
        args.append(p)
    if not zero_init:
        in_specs.append(_batch_spec(state_in_shape))
        args.append(state0)
    return pl.pallas_call(
        functools.partial(body, L, zero_init),
        grid=(nseq,),
        in_specs=in_specs,
        out_specs=[_seq_specs(L, DG, 0), _batch_spec(state_out_shape)],
        out_shape=[jax.ShapeDtypeStruct((nseq * L, DG), F32),
                   jax.ShapeDtypeStruct((nseq,) + state_out_shape, F32)],
        scratch_shapes=[pltpu.VMEM((L, w), F32) for w in n_scratch_rows] + list(extra_scratch),
        compiler_params=_cparams(1), name=name + ("_ctx" if zero_init else "_lat"),
    )(*args)


RW_SCRATCH = 14


def _rwkv_body(seq_len, zero_init, *refs):
    if zero_init:
        zc_ref, pv_ref, wup_ref, aup_ref, gup_ref, o_ref, sf_ref = refs[:7]
        s0_ref = None
        scr = refs[7:]
    else:
        zc_ref, pv_ref, wup_ref, aup_ref, gup_ref, s0_ref, o_ref, sf_ref = refs[:8]
        scr = refs[8:]
    (kt_ref, rt_ref, kh_ref, bh_ref, kb_ref, bb_ref, tot_ref, v_ref, w_ref, u_ref, rh_ref, yh_ref,
     acc_ref, yf_ref, s_ref) = scr
    L = seq_len
    n_chunks = L // RW_CHUNK
    n_groups = L // RW_GROUP
    ones = _block_ones(DG, HD).astype(BF16)
    bd = _block_ones(DG, HD).astype(F32)
    rows = _iota((L, DG), 0)

    def pvr(i):
        return pv_ref[i:i + 1, :]

    def tshift(t, mu):
        prev = jnp.where(rows >= 1, _shift(t, 1), 0.0)
        nxt = jnp.where(rows < L - 1, _shift(t, -1), 0.0)
        return t + mu * (0.5 * (prev + nxt) - t)

    r = tshift(zc_ref[:, 0:DG], pvr(0))
    k = tshift(zc_ref[:, DG:2 * DG], pvr(1))
    v = tshift(zc_ref[:, 2 * DG:3 * DG], pvr(2))
    tail = zc_ref[:, 3 * DG:3 * DG + 128]
    kk = k * pvr(7)
    kk = kk / jnp.maximum(jnp.sqrt(_head_sum(kk * kk, ones)), 1e-12)
    gate = _mm(_sigmoid(tail), gup_ref[...])
    tanh_tail = jnp.tanh(tail)
    v_ref[...] = v
    acc_ref[...] = jnp.zeros((L, DG), F32)

    n_big = 4 * RW_GROUP
    ri, ci = _iota((n_big, n_big), 0), _iota((n_big, n_big), 1)
    same = ((ri // RW_GROUP) == (ci // RW_GROUP)) & (((ri % RW_GROUP) // RW_CHUNK) == ((ci % RW_GROUP) // RW_CHUNK))
    eye = (ri == ci).astype(F32)
    lane_head = _iota((RW_GROUP, DG), 1) // HD

    def pair_blocks(b):
        return ((ri // (2 * b)) == (ci // (2 * b))) & ((ri // b) != (ci // b))

    def stack_heads(x):
        return jnp.concatenate([jnp.where(lane_head == h, x, 0.0) for h in range(4)], axis=0)

    def fold_heads(x):
        return (x[0:RW_GROUP] + x[RW_GROUP:2 * RW_GROUP]
                + x[2 * RW_GROUP:3 * RW_GROUP] + x[3 * RW_GROUP:4 * RW_GROUP])

    for d in range(2):
        w_log = -_softplus(-(pvr(3 + d) + _mm(tanh_tail, wup_ref[d]))) - 0.5
        lw = -jnp.exp(w_log)
        a = _sigmoid(pvr(5 + d) + _mm(tail, aup_ref[d]))
        kd = k * (1.0 + (a - 1.0) * pvr(8))
        beta = kk * a
        acc_ref[...] += _head_sum(r * kd * pvr(9), ones) * v
        fc, rc = _chunk_cumsums(lw, RW_CHUNK)
        lam, rest = (fc, rc - lw) if d == 0 else (rc, fc - lw)
        kt_ref[...] = kk * jnp.exp(lam - lw)
        rt_ref[...] = r * jnp.exp(lam)
        inv = jnp.exp(-lam)
        kh_ref[...] = kd * inv
        bh_ref[...] = beta * inv
        ex_rest = jnp.exp(rest)
        kb_ref[...] = kd * ex_rest
        bb_ref[...] = beta * ex_rest
        tot_ref[...] = fc + rc - lw

        earlier = (ci < ri) if d == 0 else (ci > ri)
        m_strict = (same & earlier).astype(F32)
        m_incl = (same & (earlier | (ri == ci))).astype(F32)

        def local_step(g, carry):
            g0 = pl.multiple_of(g * RW_GROUP, RW_GROUP)
            sl = pl.ds(g0, RW_GROUP)
            lhs_k = stack_heads(kt_ref[sl, :])
            lhs_r = stack_heads(rt_ref[sl, :])
            rhs_b = jnp.concatenate([bh_ref[sl, :]] * 4, axis=0)
            rhs_k = jnp.concatenate([kh_ref[sl, :]] * 4, axis=0)
            v_big = stack_heads(v_ref[sl, :])
            n_mat = m_strict * _mm_nt(lhs_k, rhs_b)
            m_mat = m_strict * _mm_nt(lhs_k, rhs_k)
            a_rb = m_incl * _mm_nt(lhs_r, rhs_b)
            a_rk = m_incl * _mm_nt(lhs_r, rhs_k)
            t_mat = eye - jnp.where(pair_blocks(1), n_mat, 0.0)
            b = 2
            while b < RW_CHUNK:
                t_mat = t_mat - _mm(t_mat, _mm(jnp.where(pair_blocks(b), n_mat, 0.0), t_mat))
                b *= 2
            w_big = _mm(t_mat, lhs_k)
            u_big = _mm(t_mat, _mm(m_mat, v_big))
            w_ref[sl, :] = fold_heads(w_big)
            u_ref[sl, :] = fold_heads(u_big)
            rh_ref[sl, :] = fold_heads(lhs_r - _mm(a_rb, w_big))
            yh_ref[sl, :] = fold_heads(_mm(a_rk, v_big) - _mm(a_rb, u_big))
            return carry
        lax.fori_loop(0, n_groups, local_step, 0, unroll=2)

        if zero_init:
            s_ref[...] = jnp.zeros((DG, DG), F32)
        else:
            s_ref[...] = s0_ref[d]

        def chunk_step(c, carry, d=d):
            cidx = c if d == 0 else n_chunks - 1 - c
            r0 = pl.multiple_of(cidx * RW_CHUNK, RW_CHUNK)
            sl = pl.ds(r0, RW_CHUNK)
            s = s_ref[...]
            yh_ref[sl, :] = _mm_nt(rh_ref[sl, :], s) + yh_ref[sl, :]
            bb = bb_ref[sl, :].astype(BF16)
            phi = _mm_tn(w_ref[sl, :].astype(BF16), bb)
            delta = _mm_tn(jnp.concatenate([v_ref[sl, :], u_ref[sl, :]], axis=0).astype(BF16),
                           jnp.concatenate([kb_ref[sl, :].astype(BF16), -bb], axis=0))
            s_ref[...] = s * jnp.exp(tot_ref[pl.ds(r0, 1), :]) + bd * (delta - _mm(s, phi))
            return carry
        lax.fori_loop(0, n_chunks, chunk_step, 0, unroll=2)
        sf_ref[d] = _fold_diag_blocks(s_ref[...])
        if d == 0:
            yf_ref[...] = yh_ref[...]
    y = yf_ref[...] + yh_ref[...]
    mu = _head_sum(y, ones) * (1.0 / HD)
    yc = y - mu
    var = _head_sum(yc * yc, ones) * (1.0 / HD)
    y = yc * lax.rsqrt(var + C_LN_EPS) * pvr(10) + pvr(11)
    o_ref[...] = (y + acc_ref[...]) * gate


S5_BLK = 256


def _s5_body(seq_len, zero_init, *refs):
    if zero_init:
        zd_ref, pa_ref, pv_ref, bre_ref, bim_ref, cre_ref, cim_ref, glu_ref, o_ref, xf_ref = refs[:10]
        x0_ref = None
        scr = refs[10:]
    else:
        (zd_ref, pa_ref, pv_ref, bre_ref, bim_ref, cre_ref, cim_ref, glu_ref, x0_ref,
         o_ref, xf_ref) = refs[:11]
        scr = refs[11:]
    xr_ref, xi_ref, acc_ref = scr
    L = seq_len
    n_chunks = L // SC_CHUNK
    n_blk = L // S5_BLK
    jb = _iota((S5_BLK, S5_N), 0) % SC_CHUNK
    j8 = _iota((SC_CHUNK, S5_N), 0)

    def cmul(ar, ai, br, bi):
        return ar * br - ai * bi, ar * bi + ai * br

    for d in range(2):
        a_re, a_im, ls = pa_ref[d, 0:1, :], pa_ref[d, 1:2, :], pa_ref[d, 2:3, :]
        step = jnp.exp(ls)
        mag = jnp.exp(step * a_re)
        p1r, p1i = mag * jnp.cos(step * a_im), mag * jnp.sin(step * a_im)
        den = a_re * a_re + a_im * a_im
        zr = ((p1r - 1.0) * a_re + p1i * a_im) / den
        zi = (p1i * a_re - (p1r - 1.0) * a_im) / den
        p2r, p2i = cmul(p1r, p1i, p1r, p1i)
        p4r, p4i = cmul(p2r, p2i, p2r, p2i)
        powers = ((1, p1r, p1i), (2, p2r, p2i), (4, p4r, p4i))
        pwr = jnp.broadcast_to(p1r, (SC_CHUNK, S5_N))
        pwi = jnp.broadcast_to(p1i, (SC_CHUNK, S5_N))
        for s, _, _ in powers:
            sgn = s if d == 0 else -s
            valid = (j8 >= s) if d == 0 else (j8 < SC_CHUNK - s)
            nr, ni = cmul(pwr, pwi, _shift(pwr, sgn), _shift(pwi, sgn))
            pwr, pwi = jnp.where(valid, nr, pwr), jnp.where(valid, ni, pwi)

        def blk_step(bi, carry, d=d, zr=zr, zi=zi, powers=powers):
            sl = pl.ds(pl.multiple_of(bi * S5_BLK, S5_BLK), S5_BLK)
            u = zd_ref[sl, :]
            m1 = _mm(u, bre_ref[d])
            m2 = _mm(u, bim_ref[d])
            xr = zr * m1 - zi * m2
            xi = zr * m2 + zi * m1
            for s, pr, pi in powers:
                sgn = s if d == 0 else -s
                valid = (jb >= s) if d == 0 else (jb < SC_CHUNK - s)
                sr = jnp.where(valid, _shift(xr, sgn), 0.0)
                si = jnp.where(valid, _shift(xi, sgn), 0.0)
                xr, xi = xr + pr * sr - pi * si, xi + pr * si + pi * sr
            xr_ref[sl, :] = xr
            xi_ref[sl, :] = xi
            return carry
        lax.fori_loop(0, n_blk, blk_step, 0)

        if zero_init:
            c0 = (jnp.zeros((1, S5_N), F32), jnp.zeros((1, S5_N), F32))
        else:
            c0 = (x0_ref[d, 0:1, :], x0_ref[d, 1:2, :])

        def chunk_step(c, carry, d=d, pwr=pwr, pwi=pwi):
            ci = c if d == 0 else n_chunks - 1 - c
            sl = pl.ds(pl.multiple_of(ci * SC_CHUNK, SC_CHUNK), SC_CHUNK)
            cr, cim = carry
            hr = xr_ref[sl, :] + pwr * cr - pwi * cim
            hi = xi_ref[sl, :] + pwr * cim + pwi * cr
            xr_ref[sl, :] = hr
            xi_ref[sl, :] = hi
            last = SC_CHUNK - 1 if d == 0 else 0
            return hr[last:last + 1, :], hi[last:last + 1, :]
        fr, fi = lax.fori_loop(0, n_chunks, chunk_step, c0)
        xf_ref[d, 0:1, :] = fr
        xf_ref[d, 1:2, :] = fi

        def out_step(bi, carry, d=d):
            sl = pl.ds(pl.multiple_of(bi * S5_BLK, S5_BLK), S5_BLK)
            y = _mm(xr_ref[sl, :], cre_ref[d]) - _mm(xi_ref[sl, :], cim_ref[d])
            if d == 0:
                acc_ref[sl, :] = y
            else:
                acc_ref[sl, :] += y
            return carry
        lax.fori_loop(0, n_blk, out_step, 0)

    y = _gelu(acc_ref[...] + pv_ref[0:1, :] * zd_ref[...])
    o_ref[...] = y * _sigmoid(_mm(y, glu_ref[...]) + pv_ref[1:2, :])


def _outproj_body(*refs):
    ctx_refs, lat_refs = refs[0:4], refs[4:8]
    x_ref, mod_ref, mg_ref, w_ref, g2_ref, rw_ref, rb_ref, x1_ref, h2_ref, route_ref, wbf_ref = refs[8:]
    i = pl.program_id(0)

    @pl.when(i == 0)
    def _():
        wbf_ref[...] = w_ref[...].astype(BF16)

    r = _mod_row(i)
    is_ctx = i * TM < T_CTX
    y = jnp.zeros((TM, D_MODEL), F32)
    for n in range(4):
        on = _rms(jnp.where(is_ctx, ctx_refs[n][...], lat_refs[n][...])) * mg_ref[:, n * DG:(n + 1) * DG]
        y = y + jnp.dot(on.astype(BF16), wbf_ref[n * DG:(n + 1) * DG, :], preferred_element_type=F32)
    x1 = x_ref[...] + _mod_chunk(mod_ref, r, 2) * y
    x1_ref[...] = x1
    h2 = _rms(x1) * g2_ref[...] * (1.0 + _mod_chunk(mod_ref, r, 4)) + _mod_chunk(mod_ref, r, 3)
    h2_ref[...] = h2.astype(BF16)
    logits = jnp.dot(h2, rw_ref[...], precision=lax.Precision.HIGHEST,
                     preferred_element_type=F32) + rb_ref[...]
    lane = _iota((TM, N_EXPERTS), 1).astype(F32)
    col = _iota((TM, 2 * TOP_K), 1)
    work = logits
    top = jnp.max(work, axis=-1, keepdims=True)
    route = jnp.zeros((TM, 2 * TOP_K), F32)
    denom = jnp.zeros((TM, 1), F32)
    for k in range(TOP_K):
        m = jnp.max(work, axis=-1, keepdims=True)
        idx = jnp.min(jnp.where(work == m, lane, float(N_EXPERTS)), axis=-1, keepdims=True)
        pe = jnp.exp(m - top)
        route = route + jnp.where(col == k, pe, 0.0) + jnp.where(col == TOP_K + k, idx, 0.0)
        denom = denom + pe
        work = jnp.where(lane == idx, -jnp.inf, work)
    route_ref[...] = jnp.where(col < TOP_K, route / denom, route)


def _outproj(layer, o_ctx, o_lat, x, mod, mg, w_out, g2, router_w, rb):
    row = lambda i: (i, 0)
    full = lambda i: (0, 0)
    n_ctx_tiles = T_CTX // TM
    ctx_row = lambda i: (jnp.minimum(i, n_ctx_tiles - 1), 0)
    lat_row = lambda i: (jnp.maximum(i - n_ctx_tiles, 0), 0)
    return pl.pallas_call(
        _outproj_body,
        grid=(T_ALL // TM,),
        in_specs=[pl.BlockSpec((TM, DG), ctx_row)] * 4 + [pl.BlockSpec((TM, DG), lat_row)] * 4 + [
            pl.BlockSpec((TM, D_MODEL), row),
            _mod_spec(layer),
            pl.BlockSpec((1, D_MODEL), full),
            pl.BlockSpec((None, D_MODEL, D_MODEL), lambda i: (layer, 0, 0)),
            pl.BlockSpec((1, D_MODEL), full),
            pl.BlockSpec((None, D_MODEL, N_EXPERTS), lambda i: (layer, 0, 0)),
            pl.BlockSpec((1, N_EXPERTS), full)],
        out_specs=[pl.BlockSpec((TM, D_MODEL), row), pl.BlockSpec((TM, D_MODEL), row),
                   pl.BlockSpec((TM, 2 * TOP_K), row)],
        out_shape=[jax.ShapeDtypeStruct((T_ALL, D_MODEL), F32),
                   jax.ShapeDtypeStruct((T_ALL, D_MODEL), BF16),
                   jax.ShapeDtypeStruct((T_ALL, 2 * TOP_K), F32)],
        scratch_shapes=[pltpu.VMEM((D_MODEL, D_MODEL), BF16)],
        compiler_params=_cparams(1), name="outproj_router",
    )(*o_ctx, *o_lat, x, mod, mg.reshape(1, D_MODEL), w_out, g2.reshape(1, D_MODEL), router_w,
      rb.reshape(1, N_EXPERTS))


def _route_tables(route):
    prob = route[:, :TOP_K].reshape(N_PARTS, MOE_SLOTS)
    eid = route[:, TOP_K:].astype(jnp.int32).reshape(N_PARTS, MOE_SLOTS)
    slot = jnp.arange(MOE_SLOTS, dtype=jnp.int32)
    experts = jnp.arange(N_EXPERTS, dtype=jnp.int32)
    cnt = jnp.sum((eid[:, :, None] == experts[None, None, :]).astype(jnp.int32), axis=1)
    zero = jnp.zeros((N_PARTS, 1), jnp.int32)
    toffs = jnp.concatenate([zero, jnp.cumsum((cnt + MOE_TILE - 1) // MOE_TILE, axis=1)], axis=1)
    stride = MOE_SLOTS + MOE_TILE
    last = N_EXPERTS * stride
    n_dummy = MOE_TILE - 1
    dummy_j = jnp.arange(n_dummy, dtype=jnp.int32)
    need = (-cnt) % MOE_TILE
    dummy_key = jnp.where(dummy_j[None, None, :] < need[:, :, None],
                          experts[None, :, None] * stride + MOE_SLOTS + dummy_j[None, None, :], last)
    n_fill = MOE_TILES * MOE_TILE - MOE_SLOTS - N_EXPERTS * n_dummy
    keys = jnp.concatenate([eid * stride + slot[None, :], dummy_key.reshape(N_PARTS, N_EXPERTS * n_dummy),
                            jnp.full((N_PARTS, n_fill), last, jnp.int32)], axis=1)
    n_pad = MOE_TILES * MOE_TILE - MOE_SLOTS
    wgt = jnp.concatenate([prob, jnp.zeros((N_PARTS, n_pad), F32)], axis=1)
    keys, wgt = lax.sort((keys, wgt), dimension=1, num_keys=1)
    in_slot = keys % stride
    tok = jnp.where((in_slot < MOE_SLOTS) & (keys < last), in_slot // TOP_K, -1)
    return (toffs.reshape(-1), tok.reshape(N_PARTS * MOE_TILES, MOE_TILE),
            wgt.reshape(N_PARTS * MOE_TILES, MOE_TILE))


def _moe_body(toffs_ref, h_ref, tok_ref, wgt_ref, lane_ref, row_ref, eye_ref, wgu_ref, bgu_ref, wdn_ref, bdn_ref,
              o_ref):
    p, e = pl.program_id(0), pl.program_id(1)

    @pl.when(e == 0)
    def _():
        o_ref[...] = jnp.zeros((MOE_PART, D_MODEL), F32)

    ones = jnp.ones((MOE_TILE, MOE_TILE), BF16)

    def to_column(row):
        return jnp.dot((eye_ref[...] * row).astype(BF16), ones, preferred_element_type=F32)

    def tile_step(ti, carry):
        tokr = tok_ref[pl.ds(ti, 1), :].astype(F32)
        hi = jnp.floor(tokr * (1.0 / 64.0))
        tok_col = 64.0 * to_column(hi) + to_column(tokr - 64.0 * hi)
        tok_col = jnp.concatenate([tok_col] * (MOE_PART // MOE_TILE), axis=1)
        gather = (lane_ref[...] == tok_col).astype(BF16)
        xs = jnp.dot(gather, h_ref[...], preferred_element_type=F32).astype(BF16)
        gu = jnp.dot(xs, wgu_ref[...], preferred_element_type=F32) + bgu_ref[...]
        gate = jnp.minimum(gu[:, :D_FF], SWIGLU_LIMIT)
        up = jnp.clip(gu[:, D_FF:], -SWIGLU_LIMIT, SWIGLU_LIMIT)
        act = (up + 1.0) * gate * _sigmoid(SWIGLU_ALPHA * gate)
        ys = jnp.dot(act.astype(BF16), wdn_ref[...], preferred_element_type=F32) + bdn_ref[...]
        wr = wgt_ref[pl.ds(ti, 1), :]
        w_hi = wr.astype(BF16).astype(F32)
        w_col = to_column(w_hi) + to_column(wr - w_hi)
        ysw = (ys * jnp.concatenate([w_col] * (D_MODEL // MOE_TILE), axis=1)).astype(BF16)
        scatter = (row_ref[...] == tokr).astype(BF16)
        for c in range(MOE_PART // 512):
            o_ref[c * 512:(c + 1) * 512, :] += jnp.dot(scatter[c * 512:(c + 1) * 512, :], ysw,
                                                      preferred_element_type=F32)
        return carry

    lax.fori_loop(toffs_ref[p * (N_EXPERTS + 1) + e], toffs_ref[p * (N_EXPERTS + 1) + e + 1], tile_step, 0)


def _moe(layer, h2, toffs, tok, wgt, w_gu, b_gu, w_dn, b_dn):
    lane_ids = lax.broadcasted_iota(jnp.int32, (MOE_TILE, MOE_PART), 1).astype(F32)
    row_ids = lax.broadcasted_iota(jnp.int32, (MOE_PART, MOE_TILE), 0).astype(F32)
    eye = jnp.eye(MOE_TILE, dtype=F32)
    const = lambda p, e, t: (0, 0)
    grid_spec = pltpu.PrefetchScalarGridSpec(
        num_scalar_prefetch=1,
        grid=(N_PARTS, N_EXPERTS),
        in_specs=[pl.BlockSpec((MOE_PART, D_MODEL), lambda p, e, t: (p, 0)),
                  pl.BlockSpec((MOE_TILES, MOE_TILE), lambda p, e, t: (p, 0)),
                  pl.BlockSpec((MOE_TILES, MOE_TILE), lambda p, e, t: (p, 0)),
                  pl.BlockSpec((MOE_TILE, MOE_PART), const),
                  pl.BlockSpec((MOE_PART, MOE_TILE), const),
                  pl.BlockSpec((MOE_TILE, MOE_TILE), const),
                  pl.BlockSpec((None, None, D_MODEL, 2 * D_FF), lambda p, e, t: (layer, e, 0, 0)),
                  pl.BlockSpec((None, None, 1, 2 * D_FF), lambda p, e, t: (layer, e, 0, 0)),
                  pl.BlockSpec((None, None, D_FF, D_MODEL), lambda p, e, t: (layer, e, 0, 0)),
                  pl.BlockSpec((None, None, 1, D_MODEL), lambda p, e, t: (layer, e, 0, 0))],
        out_specs=pl.BlockSpec((MOE_PART, D_MODEL), lambda p, e, t: (p, 0)),
    )
    return pl.pallas_call(
        _moe_body,
        grid_spec=grid_spec,
        out_shape=jax.ShapeDtypeStruct((T_ALL, D_MODEL), F32),
        compiler_params=_cparams(2), name="moe",
    )(toffs, h2, tok, wgt, lane_ids, row_ids, eye, w_gu, b_gu.reshape(DEPTH, N_EXPERTS, 1, 2 * D_FF), w_dn,
      b_dn.reshape(DEPTH, N_EXPERTS, 1, D_MODEL))


def _final_body(x1_ref, m_ref, mod_ref, g_ref, o_ref):
    r = _mod_row(pl.program_id(0))
    o_ref[...] = _rms(x1_ref[...] + _mod_chunk(mod_ref, r, 5) * m_ref[...]) * g_ref[...]


def _final_norm(x1, moe_out, mod, g):
    tok_spec = pl.BlockSpec((TM, D_MODEL), lambda i: (i, 0))
    return pl.pallas_call(
        _final_body,
        grid=(T_ALL // TM,),
        in_specs=[tok_spec, tok_spec, _mod_spec(DEPTH - 1), pl.BlockSpec((1, D_MODEL), lambda i: (0, 0))],
        out_specs=tok_spec,
        out_shape=jax.ShapeDtypeStruct((T_ALL, D_MODEL), F32),
        compiler_params=_cparams(1), name="final_norm",
    )(x1, moe_out, mod, g.reshape(1, D_MODEL))


def _to_block_diag(s):
    eye = jnp.eye(4, dtype=s.dtype)
    out = jnp.einsum('hg,...hab->...hagb', eye, s)
    return out.reshape(s.shape[:-3] + (DG, DG))


def _pad_rows(w, first, total=128):
    pad = [(0, 0)] * (w.ndim - 2) + [(first, total - first - w.shape[-2]), (0, 0)]
    return jnp.pad(w, pad)


def kernel(x_prompt, x_sample, c, state_hgrn, state_lru, state_rwkv, state_s5, c_ctx, ada_w, ada_b, norm1_g, norm2_g, w_in, w_out, merge_g, hgrn_lb_logits, hgrn_norm_g, lru_conv_w, lru_conv_b, lru_wa, lru_ba, lru_wx, lru_bx, lru_lambda, rwkv_mu, rwkv_w0, rwkv_w_up, rwkv_a0, rwkv_a_up, rwkv_g_up, rwkv_k_k, rwkv_k_a, rwkv_r_k, rwkv_ln_g, rwkv_ln_b, s5_a_re, s5_a_im, s5_log_step, s5_b_re, s5_b_im, s5_c_re, s5_c_im, s5_d, s5_glu_w, s5_glu_b, router_w, router_b, moe_w_gu, moe_b_gu, moe_w_down, moe_b_down, final_norm_g):
    cvec = jnp.concatenate([c_ctx[None], c, jnp.zeros((5, D_MODEL), F32)], axis=0)
    x_cat = jnp.concatenate([x_prompt.reshape(T_CTX, D_MODEL),
                             x_sample.reshape(N_LAT_SEQ * LAT_LEN, D_MODEL)], axis=0)
    lbl = hgrn_lb_logits.reshape(DEPTH, 2 * DG)
    hg_s0 = _to_block_diag(jnp.swapaxes(state_hgrn, -1, -2))
    rw_s0 = _to_block_diag(state_rwkv)
    s5_x0 = jnp.moveaxis(state_s5.reshape(N_LAT_SEQ, DEPTH, 2, S5_N, 2), -1, -2)
    lru_pv = jnp.concatenate([lru_conv_w, lru_conv_b[:, None], lru_ba, lru_bx, lru_lambda,
                              jnp.zeros((DEPTH, 5, DG), F32)], axis=1)
    lru_wa_bd = _to_block_diag(lru_wa)
    lru_wx_bd = _to_block_diag(lru_wx)
    rw_pv = jnp.concatenate([rwkv_mu, rwkv_w0, rwkv_a0, rwkv_k_k[:, None], rwkv_k_a[:, None],
                             rwkv_r_k[:, None], rwkv_ln_g[:, None], rwkv_ln_b[:, None],
                             jnp.zeros((DEPTH, 4, DG), F32)], axis=1)
    rw_wup = _pad_rows(rwkv_w_up, 0)
    rw_aup = _pad_rows(rwkv_a_up, 32)
    rw_gup = _pad_rows(rwkv_g_up, 64)
    s5_pa = jnp.stack([s5_a_re.reshape(DEPTH, 2, S5_N), s5_a_im.reshape(DEPTH, 2, S5_N),
                       jnp.repeat(s5_log_step, 64, axis=-1)], axis=2)
    s5_pa = jnp.pad(s5_pa, ((0, 0), (0, 0), (0, 5), (0, 0)))
    s5_pv = jnp.concatenate([s5_d[:, None], s5_glu_b[:, None], jnp.zeros((DEPTH, 6, DG), F32)], axis=1)
    eye16 = jnp.eye(16, dtype=F32)
    s5_bre = jnp.einsum('gh,ldgpc->ldgchp', eye16, s5_b_re).reshape(DEPTH, 2, DG, S5_N)
    s5_bim = jnp.einsum('gh,ldgpc->ldgchp', eye16, s5_b_im).reshape(DEPTH, 2, DG, S5_N)
    s5_cre = jnp.einsum('gh,ldgcp->ldgphc', eye16, s5_c_re).reshape(DEPTH, 2, S5_N, DG)
    s5_cim = jnp.einsum('gh,ldgcp->ldgphc', eye16, s5_c_im).reshape(DEPTH, 2, S5_N, DG)

    mod = _ada_table(cvec, ada_w, ada_b)
    x1 = _embed(x_cat, _grid_pos_table())
    moe_out = None
    w_gu_bf = moe_w_gu.astype(BF16)
    w_dn_bf = moe_w_down.astype(BF16)

    new_hgrn, new_lru, new_rwkv, new_s5 = [], [], [], []
    for l in range(DEPTH):
        x, za, zb, zc, zd = _inproj(l, x1, moe_out, mod, norm1_g[l], w_in)

        hg_body = functools.partial(_hgrn_body, l)
        hg_params = (lbl, hgrn_norm_g[l].reshape(1, DG))
        hg_extra = (pltpu.VMEM((2, DG, DG), F32),)
        hg_call = functools.partial(_mixer_call, hg_body, "hgrn", za, A_COLS, hg_params)
        hg_tail = ((2, DG, DG), (2, HD, DG), (DG,) * HG_SCRATCH, hg_extra)
        oa, sa = hg_call(None, *hg_tail)
        oa_lat, _ = hg_call(hg_s0[:, l], *hg_tail)

        lru_params = (lru_pv[l], lru_wa_bd[l], lru_wx_bd[l])
        lru_call = functools.partial(_mixer_call, _lru_body, "lru", zb, B_COLS, lru_params)
        lru_tail = ((2, DG), (2, DG), (DG, DG, DG), ())
        ob, sb = lru_call(None, *lru_tail)
        ob_lat, _ = lru_call(state_lru[:, l], *lru_tail)

        rw_params = (rw_pv[l], rw_wup[l], rw_aup[l], rw_gup[l])
        rw_call = functools.partial(_mixer_call, _rwkv_body, "rwkv", zc, C_COLS, rw_params)
        rw_tail = ((2, DG, DG), (2, HD, DG), (DG,) * RW_SCRATCH, (pltpu.VMEM((DG, DG), F32),))
        oc, sc = rw_call(None, *rw_tail)
        oc_lat, _ = rw_call(rw_s0[:, l], *rw_tail)

        s5_params = (s5_pa[l], s5_pv[l], s5_bre[l], s5_bim[l], s5_cre[l], s5_cim[l], s5_glu_w[l])
        s5_call = functools.partial(_mixer_call, _s5_body, "s5", zd, D_COLS, s5_params)
        s5_tail = ((2, 2, S5_N), (2, 2, S5_N), (S5_N, S5_N, DG), ())
        od, sd = s5_call(None, *s5_tail)
        od_lat, _ = s5_call(s5_x0[:, l], *s5_tail)

        x1, h2, route = _outproj(l, (oa, ob, oc, od), (oa_lat, ob_lat, oc_lat, od_lat), x, mod, merge_g[l],
                                 w_out, norm2_g[l], router_w, router_b[l])
        toffs, tok, wgt = _route_tables(route)
        moe_out = _moe(l, h2, toffs, tok, wgt, w_gu_bf, moe_b_gu, w_dn_bf, moe_b_down)
        new_hgrn.append(sa)
        new_lru.append(sb)
        new_rwkv.append(sc)
        new_s5.append(sd)

    y = _final_norm(x1, moe_out, mod, final_norm_g)
    y_prompt = y[:T_CTX].reshape(N_CTX_SEQ, CTX_LEN, D_MODEL)
    y_sample = y[T_CTX:].reshape(N_LAT_SEQ, LAT_LEN, D_MODEL)
    st_shape = (N_CTX_SEQ, DEPTH, 2, HD, 4, HD)
    st_hgrn = jnp.transpose(jnp.stack(new_hgrn, axis=1).reshape(st_shape), (0, 1, 2, 4, 5, 3))
    st_lru = jnp.stack(new_lru, axis=1)
    st_rwkv = jnp.transpose(jnp.stack(new_rwkv, axis=1).reshape(st_shape), (0, 1, 2, 4, 3, 5))
    st_s5 = jnp.moveaxis(jnp.stack(new_s5, axis=1), -2, -1).reshape(N_CTX_SEQ, DEPTH, 2, 16, 64, 2)
    return (y_prompt, y_sample, st_hgrn, st_lru, st_rwkv, st_s5)
```
